```python
import jax, jax.numpy as jnp
from jax import lax
import numpy as np

D_MODEL = 1024
BATCH = 32
SEQ = 256
DEPTH = 2
DEC_BATCH = 4
DEC_SEQ = 4096
PAST_LEN = 256

GRID_W = 64
N_EVEN = (DEPTH + 1) // 2
N_ODD = DEPTH // 2
N_MOD = 9
D_FF = 2816
EPS = 1e-6
CONV_CH = D_MODEL // 2
CONV_W = 31
HG_HEADS = 4
HG_KDIM = 128
HG_VDIM = 128
HG_WIDTH = HG_HEADS * HG_KDIM
HG_CHUNK = 64
EVEN_IN = 2 * CONV_CH + 5 * HG_WIDTH
EVEN_MIX = CONV_CH + HG_HEADS * HG_VDIM
N_HEADS = 16
N_KV_HEADS = 4
HEAD_DIM = 64
GROUP = N_HEADS // N_KV_HEADS
WINDOW = 128
BLOCK = 128
ODD_IN = (N_HEADS + 2 * N_KV_HEADS) * HEAD_DIM
ROPE_AX = HEAD_DIM // 2
ROPE_BASE = 10000.0

kernel_name = 'hybrid_diffusion_conv_hgrn2_swa_step'

F32 = jnp.float32


def rms_norm(x, g):
    xf = x.astype(F32)
    y = xf * lax.rsqrt(jnp.mean(xf * xf, axis=-1, keepdims=True) + EPS)
    return (y * g.astype(F32)).astype(x.dtype)


def layer_norm(x, g, b):
    xf = x.astype(F32)
    mu = jnp.mean(xf, axis=-1, keepdims=True)
    var = jnp.mean(jnp.square(xf - mu), axis=-1, keepdims=True)
    return ((xf - mu) * lax.rsqrt(var + 1e-5) * g.astype(F32) + b.astype(F32)).astype(x.dtype)


def adaln_params(cond, w, b):
    m = jax.nn.silu(cond) @ w + b
    return m.reshape(cond.shape[0], N_MOD, 1, D_MODEL)


def pre_norm(x, g, shift, scale):
    return rms_norm(x, g) * (1 + scale) + shift


def swiglu(h, w_in, w_out):
    gt, up = jnp.split(h @ w_in, 2, axis=-1)
    return (jax.nn.silu(gt) * up) @ w_out


def ffn_sublayer(x, mod, g, w_in, w_out, slot):
    h = pre_norm(x, g, mod[:, 3 * slot], mod[:, 3 * slot + 1])
    return x + 0.5 * mod[:, 3 * slot + 2] * swiglu(h, w_in, w_out)


def conformer_conv(u, w_dw, b_dw, ln_g, ln_b):
    a = u[..., :CONV_CH] * jax.nn.sigmoid(u[..., CONV_CH:])
    y = lax.conv_general_dilated(
        a, w_dw[:, None, :].astype(a.dtype), window_strides=(1,),
        padding=[(CONV_W // 2, CONV_W // 2)],
        dimension_numbers=('NWC', 'WIO', 'NWC'), feature_group_count=CONV_CH) + b_dw
    return jax.nn.silu(layer_norm(y, ln_g, ln_b))


def gla_chunk_scan(q, k, logf, v, s0):
    B, T, H, K = q.shape
    V = v.shape[-1]
    n = T // HG_CHUNK

    def blocks(a):
        return a.astype(F32).reshape(B, n, HG_CHUNK, H, a.shape[-1]).transpose(1, 0, 3, 2, 4)

    causal = jnp.tril(jnp.ones((HG_CHUNK, HG_CHUNK), bool))[:, :, None]

    def step(S, inp):
        qc, kc, gc, vc = inp
        b = jnp.cumsum(gc, axis=2)
        inter = jnp.einsum('bhtk,bhkv->bhtv', qc * jnp.exp(b), S)
        rel = jnp.exp(jnp.where(causal, b[:, :, :, None, :] - b[:, :, None, :, :], -jnp.inf))
        scores = jnp.einsum('bhtk,bhtsk,bhsk->bhts', qc, rel, kc)
        intra = jnp.einsum('bhts,bhsv->bhtv', scores, vc)
        b_end = b[:, :, -1:, :]
        S_new = jnp.exp(b_end[:, :, 0, :])[..., None] * S + jnp.einsum(
            'bhsk,bhsv->bhkv', kc * jnp.exp(b_end - b), vc)
        return S_new, inter + intra

    s_last, o = lax.scan(step, s0.astype(F32), (blocks(q), blocks(k), blocks(logf), blocks(v)))
    return o.transpose(1, 0, 3, 2, 4).reshape(B, T, H, V), s_last


def hgrn2_gate(z, lb):
    f = lb + (1 - lb) * jax.nn.sigmoid(z.astype(F32))
    return jnp.log(f), 1 - f


def hgrn2_mixer(p, lb_f, lb_b, norm_g, s0_f, s0_b):
    B, T, _ = p.shape
    q, zf, zb, i, g = jnp.split(p, 5, axis=-1)
    shp = (B, T, HG_HEADS, HG_KDIM)
    q = jax.nn.silu(q).reshape(shp)
    v = i.reshape(B, T, HG_HEADS, HG_VDIM)
    logf_f, k_f = hgrn2_gate(zf.reshape(shp), lb_f.reshape(HG_HEADS, HG_KDIM))
    logf_b, k_b = hgrn2_gate(zb.reshape(shp), lb_b.reshape(HG_HEADS, HG_KDIM))
    o_f, s_f = gla_chunk_scan(q, k_f, logf_f, v, s0_f)
    rev = lambda a: jnp.flip(a, axis=1)
    o_b, s_b = gla_chunk_scan(rev(q), rev(k_b), rev(logf_b), rev(v), s0_b)
    o = o_f + rev(o_b)
    o = rms_norm(o, norm_g) * jax.nn.silu(g.reshape(B, T, HG_HEADS, HG_VDIM).astype(F32))
    return o.reshape(B, T, HG_HEADS * HG_VDIM), s_f, s_b


def even_mixer(h, w_in, w_out, conv_w, conv_b, ln_g, ln_b, lb_f, lb_b, hg_g, s0_f, s0_b):
    u = h @ w_in
    a = conformer_conv(u[..., :2 * CONV_CH], conv_w, conv_b, ln_g, ln_b)
    r, s_f, s_b = hgrn2_mixer(u[..., 2 * CONV_CH:], lb_f, lb_b, hg_g, s0_f, s0_b)
    out = jnp.concatenate([a, r.astype(a.dtype)], axis=-1) @ w_out
    return out, s_f, s_b


def attn_qkv(h, w_in, qn, kn):
    B, T, _ = h.shape
    q, k, v = jnp.split(h @ w_in, [N_HEADS * HEAD_DIM, (N_HEADS + N_KV_HEADS) * HEAD_DIM], axis=-1)
    q = rms_norm(q.reshape(B, T, N_HEADS, HEAD_DIM), qn)
    k = rms_norm(k.reshape(B, T, N_KV_HEADS, HEAD_DIM), kn)
    return q, k, v.reshape(B, T, N_KV_HEADS, HEAD_DIM)


def axial_rope(x):
    T = x.shape[1]
    rows = T // GRID_W
    row = jnp.repeat(jnp.arange(rows), GRID_W).astype(F32)
    col = jnp.tile(jnp.arange(GRID_W), rows).astype(F32)
    inv = ROPE_BASE ** (-jnp.arange(0, ROPE_AX, 2, dtype=F32) / ROPE_AX)

    def rot(xa, pos):
        ang = pos[:, None] * inv[None, :]
        cos = jnp.cos(ang)[None, :, None, :]
        sin = jnp.sin(ang)[None, :, None, :]
        x1, x2 = jnp.split(xa.astype(F32), 2, axis=-1)
        return jnp.concatenate([x1 * cos - x2 * sin, x2 * cos + x1 * sin], axis=-1)

    out = jnp.concatenate([rot(x[..., :ROPE_AX], row), rot(x[..., ROPE_AX:], col)], axis=-1)
    return out.astype(x.dtype)


def sink_attend(s, sink, v):
    sk = sink.astype(F32)[None, :, :, None, None]
    m = jnp.maximum(jnp.max(s, axis=-1, keepdims=True), sk)
    p = jnp.exp(s - m)
    denom = jnp.sum(p, axis=-1, keepdims=True) + jnp.exp(sk - m)
    return jnp.einsum('bkgqs,bskd->bqkgd', p / denom, v.astype(F32))


def context_attention(q, k, v, sink):
    B, T = q.shape[:2]
    nb = T // BLOCK
    qb = q.reshape(B, nb, BLOCK, N_KV_HEADS, GROUP, HEAD_DIM).transpose(1, 0, 2, 3, 4, 5)
    sk = sink.reshape(N_KV_HEADS, GROUP)
    scale = HEAD_DIM ** -0.5

    def one(qblk):
        s = jnp.einsum('bqkgd,bskd->bkgqs', qblk, k).astype(F32) * scale
        return sink_attend(s, sk, v)

    o = lax.map(one, qb)
    return o.transpose(1, 0, 2, 3, 4, 5).reshape(B, T, N_HEADS * HEAD_DIM)


def latent_attention(q, k, v, ck, cv, sink):
    B, T = q.shape[:2]
    nb = T // BLOCK
    qg = q.reshape(B, T, N_KV_HEADS, GROUP, HEAD_DIM)
    pad = [(0, 0), (BLOCK, BLOCK), (0, 0), (0, 0)]
    kp = jnp.pad(k, pad)
    vp = jnp.pad(v, pad)
    qi = jnp.arange(BLOCK)[:, None]
    kj = jnp.arange(3 * BLOCK)[None, :]
    band = jnp.abs(kj - BLOCK - qi) <= WINDOW
    sk = sink.reshape(N_KV_HEADS, GROUP)
    scale = HEAD_DIM ** -0.5

    def one(n):
        start = n * BLOCK
        qblk = lax.dynamic_slice_in_dim(qg, start, BLOCK, axis=1)
        kblk = lax.dynamic_slice_in_dim(kp, start, 3 * BLOCK, axis=1)
        vblk = lax.dynamic_slice_in_dim(vp, start, 3 * BLOCK, axis=1)
        kpos = start - BLOCK + kj
        mask = band & (kpos >= 0) & (kpos < T)
        s_loc = jnp.where(mask, jnp.einsum('bqkgd,bskd->bkgqs', qblk, kblk).astype(F32) * scale, -jnp.inf)
        s_ctx = jnp.einsum('bqkgd,bskd->bkgqs', qblk, ck).astype(F32) * scale
        s = jnp.concatenate([s_ctx, s_loc], axis=-1)
        vals = jnp.concatenate([cv.astype(F32), vblk.astype(F32)], axis=1)
        return sink_attend(s, sk, vals)

    o = lax.map(one, jnp.arange(nb))
    return o.transpose(1, 0, 2, 3, 4, 5).reshape(B, T, N_HEADS * HEAD_DIM)


def setup_inputs(seed: int = 0) -> dict:
    key = jax.random.key(seed)
    ks = jax.random.split(key, 26)
    nrm = lambda k, shape, s: jax.random.normal(k, shape, F32) * s
    return {
        'x_prompt': nrm(ks[0], (BATCH, SEQ, D_MODEL), 1.0),
        'x_sample': nrm(ks[1], (DEC_BATCH, DEC_SEQ, D_MODEL), 1.0),
        'c': nrm(ks[2], (DEC_BATCH, D_MODEL), 1.0),
        'state_hgrn': nrm(ks[3], (DEC_BATCH, N_EVEN, 2, HG_HEADS, HG_KDIM, HG_VDIM), 0.5),
        'cache_k': nrm(ks[4], (DEC_BATCH, N_ODD, PAST_LEN, N_KV_HEADS, HEAD_DIM), 1.0),
        'cache_v': nrm(ks[5], (DEC_BATCH, N_ODD, PAST_LEN, N_KV_HEADS, HEAD_DIM), 1.0),
        'c_ctx': nrm(ks[6], (D_MODEL,), 1.0),
        'w_mod': nrm(ks[7], (DEPTH, D_MODEL, N_MOD * D_MODEL), 0.5 * D_MODEL ** -0.5),
        'b_mod': nrm(ks[8], (DEPTH, N_MOD * D_MODEL), 0.02),
        'norm_g': 1.0 + nrm(ks[9], (DEPTH, 3, D_MODEL), 0.05),
        'ffn_w_in': nrm(ks[10], (DEPTH, 2, D_MODEL, 2 * D_FF), D_MODEL ** -0.5),
        'ffn_w_out': nrm(ks[11], (DEPTH, 2, D_FF, D_MODEL), D_FF ** -0.5),
        'ev_w_in': nrm(ks[12], (N_EVEN, D_MODEL, EVEN_IN), D_MODEL ** -0.5),
        'ev_w_out': nrm(ks[13], (N_EVEN, EVEN_MIX, D_MODEL), EVEN_MIX ** -0.5),
        'conv_w': nrm(ks[14], (N_EVEN, CONV_W, CONV_CH), CONV_W ** -0.5),
        'conv_b': nrm(ks[15], (N_EVEN, CONV_CH), 0.02),
        'conv_ln_g': 1.0 + nrm(ks[16], (N_EVEN, CONV_CH), 0.05),
        'conv_ln_b': nrm(ks[17], (N_EVEN, CONV_CH), 0.02),
        'hg_lb_raw': nrm(ks[18], (2, DEPTH + 1, HG_WIDTH), 0.5),
        'hg_norm_g': 1.0 + nrm(ks[19], (N_EVEN, HG_VDIM), 0.05),
        'od_w_in': nrm(ks[20], (N_ODD, D_MODEL, ODD_IN), D_MODEL ** -0.5),
        'od_w_out': nrm(ks[21], (N_ODD, N_HEADS * HEAD_DIM, D_MODEL), (N_HEADS * HEAD_DIM) ** -0.5),
        'q_norm_g': 1.0 + nrm(ks[22], (N_ODD, HEAD_DIM), 0.05),
        'k_norm_g': 1.0 + nrm(ks[23], (N_ODD, HEAD_DIM), 0.05),
        'sinks': nrm(ks[24], (N_ODD, N_HEADS), 0.5),
    }


def reference(x_prompt, x_sample, c, state_hgrn, cache_k, cache_v, c_ctx, w_mod, b_mod, norm_g,
              ffn_w_in, ffn_w_out, ev_w_in, ev_w_out, conv_w, conv_b, conv_ln_g, conv_ln_b,
              hg_lb_raw, hg_norm_g, od_w_in, od_w_out, q_norm_g, k_norm_g, sinks):
    lb = jnp.cumsum(jax.nn.softmax(hg_lb_raw.astype(F32), axis=1), axis=1)

    xp = x_prompt
    hg_states, ks_new, vs_new = [], [], []
    for l in range(DEPTH):
        mod = adaln_params(c_ctx[None, :], w_mod[l], b_mod[l])
        xp = ffn_sublayer(xp, mod, norm_g[l, 0], ffn_w_in[l, 0], ffn_w_out[l, 0], 0)
        h = pre_norm(xp, norm_g[l, 1], mod[:, 3], mod[:, 4])
        if l % 2 == 0:
            e = l // 2
            s0 = jnp.zeros((xp.shape[0], HG_HEADS, HG_KDIM, HG_VDIM), F32)
            mix, s_f, s_b = even_mixer(h, ev_w_in[e], ev_w_out[e], conv_w[e], conv_b[e], conv_ln_g[e],
                                       conv_ln_b[e], lb[0, l], lb[1, l], hg_norm_g[e], s0, s0)
            hg_states.append(jnp.stack([s_f, s_b], axis=1))
        else:
            o = l // 2
            q, k, v = attn_qkv(h, od_w_in[o], q_norm_g[o], k_norm_g[o])
            mix = context_attention(q, k, v, sinks[o]).astype(h.dtype) @ od_w_out[o]
            ks_new.append(k)
            vs_new.append(v)
        xp = xp + mod[:, 5] * mix
        xp = ffn_sublayer(xp, mod, norm_g[l, 2], ffn_w_in[l, 1], ffn_w_out[l, 1], 2)
    y_prompt = xp
    new_state_hgrn = jnp.stack(hg_states, axis=1)
    new_cache_k = jnp.stack(ks_new, axis=1)
    new_cache_v = jnp.stack(vs_new, axis=1)

    xs = x_sample
    for l in range(DEPTH):
        mod = adaln_params(c, w_mod[l], b_mod[l])
        xs = ffn_sublayer(xs, mod, norm_g[l, 0], ffn_w_in[l, 0], ffn_w_out[l, 0], 0)
        h = pre_norm(xs, norm_g[l, 1], mod[:, 3], mod[:, 4])
        if l % 2 == 0:
            e = l // 2
            mix, _, _ = even_mixer(h, ev_w_in[e], ev_w_out[e], conv_w[e], conv_b[e], conv_ln_g[e],
                                   conv_ln_b[e], lb[0, l], lb[1, l], hg_norm_g[e],
                                   state_hgrn[:, e, 0], state_hgrn[:, e, 1])
        else:
            o = l // 2
            q, k, v = attn_qkv(h, od_w_in[o], q_norm_g[o], k_norm_g[o])
            q = axial_rope(q)
            k = axial_rope(k)
            mix = latent_attention(q, k, v, cache_k[:, o], cache_v[:, o], sinks[o]).astype(h.dtype) @ od_w_out[o]
        xs = xs + mod[:, 5] * mix
        xs = ffn_sublayer(xs, mod, norm_g[l, 2], ffn_w_in[l, 1], ffn_w_out[l, 1], 2)
    y_sample = xs

    return (y_prompt, y_sample, new_state_hgrn, new_cache_k, new_cache_v)
```

```python
import functools

import jax
import jax.numpy as jnp
from jax import lax
from jax.experimental import pallas as pl
from jax.experimental.pallas import tpu as pltpu

F32 = jnp.float32
BF16 = jnp.bfloat16

D_MODEL = 1024
DEPTH = 2
N_MOD = 9
D_FF = 2816
EPS = 1e-6
LN_EPS = 1e-5
CONV_CH = 512
CONV_W = 31
CONV_HALO = 16
HG_HEADS = 4
HG_KDIM = 128
HG_VDIM = 128
HG_WIDTH = HG_HEADS * HG_KDIM
HG_CHUNK = 64
HG_SUB = 8
EVEN_IN = 2 * CONV_CH + 5 * HG_WIDTH
N_HEADS = 16
N_KV_HEADS = 4
HEAD_DIM = 64
GROUP = N_HEADS // N_KV_HEADS
ATT_BLOCK = 128
GRID_W = 64
ROPE_AX = HEAD_DIM // 2
ROPE_BASE = 10000.0
LANES = 128
VMEM_LIMIT_BYTES = 56 * 1024 * 1024
ROW_TILE = 512


def _cparams(*sem):
    return pltpu.CompilerParams(dimension_semantics=sem, vmem_limit_bytes=VMEM_LIMIT_BYTES)


def _dot(a, b):
    return jnp.dot(a, b, preferred_element_type=F32)


def _dot_nt(a, b):
    return lax.dot_general(a, b, (((1,), (1,)), ((), ())), preferred_element_type=F32)


def _dot_tn(a, b):
    return lax.dot_general(a, b, (((0,), (0,)), ((), ())), preferred_element_type=F32)


def _silu(x):
    return x * jax.nn.sigmoid(x)


def _resident(shape):
    return pl.BlockSpec(shape, lambda *_: (0,) * len(shape), pipeline_mode=pl.Buffered(1))


def _prenorm(x, g, mod_ref, slot):
    shift = mod_ref[0, 3 * slot:3 * slot + 1, :]
    scale = mod_ref[0, 3 * slot + 1:3 * slot + 2, :]
    ms = jnp.mean(x * x, axis=-1, keepdims=True)
    y = x * lax.rsqrt(ms + EPS) * g
    return y * (1.0 + scale) + shift


def _mod_kernel(c_ref, w_ref, b_ref, o_ref):
    c = c_ref[...]
    o_ref[0] = _dot(_silu(c).astype(BF16), w_ref[0].astype(BF16)) + b_ref[0]


def _adaln(cond, w_mod, b_mod):
    n_layers, d, n = w_mod.shape
    r = cond.shape[0]
    tn = 1024
    return pl.pallas_call(
        _mod_kernel,
        grid=(n_layers, n // tn),
        in_specs=[pl.BlockSpec((r, d), lambda l, j: (0, 0)),
                  pl.BlockSpec((1, d, tn), lambda l, j: (l, 0, j)),
                  pl.BlockSpec((1, 1, tn), lambda l, j: (l, 0, j))],
        out_specs=pl.BlockSpec((1, r, tn), lambda l, j: (l, 0, j)),
        out_shape=jax.ShapeDtypeStruct((n_layers, r, n), F32),
        compiler_params=_cparams("arbitrary", "arbitrary"),
        name="adaln",
    )(cond, w_mod, b_mod.reshape(n_layers, 1, n))


def _ffn_chunks():
    out, c0 = [], 0
    while c0 < D_FF:
        cw = min(1024, D_FF - c0)
        out.append((c0, cw))
        c0 += cw
    return out


def _ffn_kernel(x_ref, mod_ref, g_ref, win_ref, wout_ref, o_ref, act_ref, *, slot):
    x = x_ref[...]
    h = _prenorm(x, g_ref[...], mod_ref, slot).astype(BF16)
    for c0, cw in _ffn_chunks():
        gt = _dot(h, win_ref[:, c0:c0 + cw])
        up = _dot(h, win_ref[:, D_FF + c0:D_FF + c0 + cw])
        act_ref[:, c0:c0 + cw] = (_silu(gt) * up).astype(BF16)
    out = _dot(act_ref[...], wout_ref[...])
    gate = mod_ref[0, 3 * slot + 2:3 * slot + 3, :]
    o_ref[...] = x + (0.5 * gate) * out


def _ffn(x, mod, g, w_in, w_out, slot):
    n, d = x.shape
    tm = ROW_TILE
    per_group = (n // mod.shape[0]) // tm
    return pl.pallas_call(
        functools.partial(_ffn_kernel, slot=slot),
        grid=(n // tm,),
        in_specs=[pl.BlockSpec((tm, d), lambda i: (i, 0)),
                  pl.BlockSpec((1, N_MOD, d), lambda i: (i // per_group, 0, 0)),
                  pl.BlockSpec((1, d), lambda i: (0, 0)),
                  _resident(w_in.shape),
                  _resident(w_out.shape)],
        out_specs=pl.BlockSpec((tm, d), lambda i: (i, 0)),
        out_shape=jax.ShapeDtypeStruct((n, d), F32),
        scratch_shapes=[pltpu.VMEM((tm, D_FF), BF16)],
        compiler_params=_cparams("arbitrary"),
        name="ffn",
    )(x, mod, g.reshape(1, d), w_in, w_out)


def _proj_kernel(x_ref, mod_ref, g_ref, w_ref, o_ref):
    h = _prenorm(x_ref[...], g_ref[...], mod_ref, 1).astype(BF16)
    o_ref[...] = _dot(h, w_ref[...])


def _proj(x, mod, g, w):
    n, d = x.shape
    nout = w.shape[1]
    tm = ROW_TILE
    per_group = (n // mod.shape[0]) // tm
    return pl.pallas_call(
        _proj_kernel,
        grid=(n // tm,),
        in_specs=[pl.BlockSpec((tm, d), lambda i: (i, 0)),
                  pl.BlockSpec((1, N_MOD, d), lambda i: (i // per_group, 0, 0)),
                  pl.BlockSpec((1, d), lambda i: (0, 0)),
                  _resident(w.shape)],
        out_specs=pl.BlockSpec((tm, nout), lambda i: (i, 0)),
        out_shape=jax.ShapeDtypeStruct((n, nout), F32),
        compiler_params=_cparams("arbitrary"),
        name="even_proj",
    )(x, mod, g.reshape(1, d), w)


def _conv_kernel(cur_ref, prev_ref, next_ref, w_ref, cb_ref, lg_ref, lb_ref, o_ref, ext_ref, *, tt):
    t = pl.program_id(1)
    nt = pl.num_programs(1)
    c = CONV_CH
    h = CONV_HALO

    def glu(v):
        return v[:, :c] * jax.nn.sigmoid(v[:, c:])

    ext_ref[0:h, :] = jnp.where(t > 0, glu(prev_ref[0]), 0.0)
    ext_ref[h:h + tt, :] = glu(cur_ref[0])
    ext_ref[h + tt:2 * h + tt, :] = jnp.where(t < nt - 1, glu(next_ref[0]), 0.0)
    rs = 64
    first = h - CONV_W // 2
    for r in range(0, tt, rs):
        acc = jnp.zeros((rs, c), F32)
        for j in range(CONV_W):
            acc = acc + ext_ref[r + first + j:r + first + j + rs, :] * w_ref[j:j + 1, :]
        y = acc + cb_ref[...]
        mu = jnp.mean(y, axis=-1, keepdims=True)
        dlt = y - mu
        var = jnp.mean(dlt * dlt, axis=-1, keepdims=True)
        yn = dlt * lax.rsqrt(var + LN_EPS) * lg_ref[...] + lb_ref[...]
        o_ref[0, r:r + rs, :] = _silu(yn)


def _conv_module(u, w, cb, ln_g, ln_b):
    b, t, _ = u.shape
    tt = min(t, 512)
    c = CONV_CH
    hb = tt // CONV_HALO
    last = t // CONV_HALO - 1
    wp = jnp.zeros((32, c), F32).at[:CONV_W].set(w)
    return pl.pallas_call(
        functools.partial(_conv_kernel, tt=tt),
        grid=(b, t // tt),
        in_specs=[pl.BlockSpec((1, tt, 2 * c), lambda i, j: (i, j, 0)),
                  pl.BlockSpec((1, CONV_HALO, 2 * c), lambda i, j: (i, jnp.maximum(j * hb - 1, 0), 0)),
                  pl.BlockSpec((1, CONV_HALO, 2 * c), lambda i, j: (i, jnp.minimum((j + 1) * hb, last), 0)),
                  pl.BlockSpec((32, c), lambda i, j: (0, 0)),
                  pl.BlockSpec((1, c), lambda i, j: (0, 0)),
                  pl.BlockSpec((1, c), lambda i, j: (0, 0)),
                  pl.BlockSpec((1, c), lambda i, j: (0, 0))],
        out_specs=pl.BlockSpec((1, tt, c), lambda i, j: (i, j, 0)),
        out_shape=jax.ShapeDtypeStruct((b, t, c), F32),
        scratch_shapes=[pltpu.VMEM((tt + 2 * CONV_HALO, c), F32)],
        compiler_params=_cparams("arbitrary", "arbitrary"),
        name="conv_module",
    )(u, u, u, wp, cb.reshape(1, c), ln_g.reshape(1, c), ln_b.reshape(1, c))


def _split3(x):
    hi = x.astype(BF16)
    r1 = x - hi.astype(F32)
    mid = r1.astype(BF16)
    lo = (r1 - mid.astype(F32)).astype(BF16)
    return hi, mid, lo


def _hgrn_unit(q, kk, b, bx, v, st, b_end_exp, kdec, rev, w2, sub_iota):
    c = HG_CHUNK
    sb = HG_SUB
    nb = c // sb
    inter = _dot_nt((q * jnp.exp(b)).astype(BF16), st.astype(BF16))
    rows = []
    for i in range(nb):
        r0 = i * sb
        lo, hi = (r0 + sb, c) if rev else (0, r0)
        if hi <= lo:
            rows.append(jnp.zeros((sb, c), F32))
            continue
        ref = bx[r0 + sb - 1:r0 + sb, :] if rev else bx[r0:r0 + 1, :]
        qt = q[r0:r0 + sb, :] * jnp.exp(b[r0:r0 + sb, :] - ref)
        kt = kk[lo:hi, :] * jnp.exp(ref - b[lo:hi, :])
        pieces = [kt] if not rev else [jnp.zeros((lo, HG_KDIM), F32), kt]
        if not rev and hi < c:
            pieces.append(jnp.zeros((c - hi, HG_KDIM), F32))
        kfull = jnp.concatenate(pieces, axis=0) if len(pieces) > 1 else pieces[0]
        rows.append(_dot_nt(qt.astype(BF16), kfull.astype(BF16)))
    scores = jnp.concatenate(rows, axis=0)
    o_off = _dot(scores.astype(BF16), v.astype(BF16))
    xs = []
    for i in range(nb):
        r0 = i * sb
        bq = b[r0:r0 + sb, :]
        qq = q[r0:r0 + sb, :]
        for p in range(sb // 2):
            pair = []
            for s in (2 * p, 2 * p + 1):
                keep = (sub_iota <= s) if rev else (sub_iota >= s)
                e = jnp.exp(bq - b[r0 + s:r0 + s + 1, :])
                pair.append(jnp.where(keep, qq * kk[r0 + s:r0 + s + 1, :] * e, 0.0))
            xs.append(jnp.concatenate(pair, axis=1))
    y2 = _dot(jnp.concatenate(xs, axis=0).astype(BF16), w2)
    diag = []
    for i in range(nb):
        r0 = i * sb
        acc = jnp.zeros((sb, HG_VDIM), F32)
        for p in range(sb // 2):
            g0 = (i * (sb // 2) + p) * sb
            acc = acc + y2[g0:g0 + sb, :LANES] * v[r0 + 2 * p:r0 + 2 * p + 1, :]
            acc = acc + y2[g0:g0 + sb, LANES:] * v[r0 + 2 * p + 1:r0 + 2 * p + 2, :]
        diag.append(acc)
    o = inter + o_off + jnp.concatenate(diag, axis=0)
    st_new = st * b_end_exp + _dot_tn(v.astype(BF16), kdec.astype(BF16))
    return o, st_new


def _hgrn_kernel(*refs, layer, tc, has_s0, emit_state):
    qf_ref, zf_ref, vf_ref, qb_ref, zb_ref, vb_ref, lbraw_ref = refs[:7]
    pos = 7
    s0_ref = None
    if has_s0:
        s0_ref = refs[pos]
        pos += 1
    of_ref, ob_ref = refs[pos], refs[pos + 1]
    pos += 2
    sf_ref = None
    if emit_state:
        sf_ref = refs[pos]
        pos += 1
    st_ref = refs[pos]

    s = pl.program_id(1)
    ns = pl.num_programs(1)
    c = HG_CHUNK
    nch = tc // c

    @pl.when(s == 0)
    def _init():
        for d in range(2):
            for h in range(HG_HEADS):
                if has_s0:
                    st_ref[d, h] = s0_ref[0, d, h].T
                else:
                    st_ref[d, h] = jnp.zeros((HG_VDIM, HG_KDIM), F32)

    def lower_bound(d):
        rows = [lbraw_ref[d, j:j + 1, :] for j in range(DEPTH + 1)]
        m = functools.reduce(jnp.maximum, rows)
        ex = [jnp.exp(r - m) for r in rows]
        tot = functools.reduce(lambda a, b_: a + b_, ex)
        return functools.reduce(lambda a, b_: a + b_, [e / tot for e in ex[:layer + 1]])

    lbs = (lower_bound(0), lower_bound(1))
    r64 = lax.broadcasted_iota(jnp.int32, (c, c), 0)
    c64 = lax.broadcasted_iota(jnp.int32, (c, c), 1)
    tri = ((r64 >= c64).astype(BF16), (r64 <= c64).astype(BF16))
    r256 = lax.broadcasted_iota(jnp.int32, (2 * LANES, 2 * LANES), 0)
    c256 = lax.broadcasted_iota(jnp.int32, (2 * LANES, 2 * LANES), 1)
    w2 = ((r256 >> 7) == (c256 >> 7)).astype(BF16)
    sub_iota = lax.broadcasted_iota(jnp.int32, (HG_SUB, LANES), 0)

    def chunk_body(ci, carry):
        for d, (q_ref, z_ref, v_ref, o_ref) in enumerate(
                ((qf_ref, zf_ref, vf_ref, of_ref), (qb_ref, zb_ref, vb_ref, ob_ref))):
            rev = d == 1
            off = pl.multiple_of((nch - 1 - ci) * c if rev else ci * c, c)
            q = _silu(q_ref[0, pl.ds(off, c), :])
            z = z_ref[0, pl.ds(off, c), :]
            v = v_ref[0, pl.ds(off, c), :]
            lb = lbs[d]
            f = lb + (1.0 - lb) * jax.nn.sigmoid(z)
            lf = jnp.log(f)
            kk = 1.0 - f
            parts = jnp.concatenate(_split3(lf), axis=1)
            cs = _dot(tri[d], parts)
            w = HG_WIDTH
            b = cs[:, :w] + cs[:, w:2 * w] + cs[:, 2 * w:]
            bx = b - lf
            b_end = b[0:1, :] if rev else b[c - 1:c, :]
            kdec = kk * jnp.exp(b_end - b)
            b_end_exp = jnp.exp(b_end)
            outs = []
            for h in range(HG_HEADS):
                sl = slice(h * HG_KDIM, (h + 1) * HG_KDIM)
                o, st_new = _hgrn_unit(q[:, sl], kk[:, sl], b[:, sl], bx[:, sl], v[:, sl], st_ref[d, h],
                                       b_end_exp[:, sl], kdec[:, sl], rev, w2, sub_iota)
                st_ref[d, h] = st_new
                outs.append(o)
            o_ref[0, pl.ds(off, c), :] = jnp.concatenate(outs, axis=1)
        return carry

    lax.fori_loop(0, nch, chunk_body, 0)

    if emit_state:
        @pl.when(s == ns - 1)
        def _fin():
            for d in range(2):
                for h in range(HG_HEADS):
                    sf_ref[0, d, h] = st_ref[d, h].T


def _hgrn_scan(u, lb_raw, s0, layer, emit_state):
    b, t, _ = u.shape
    tc = 256
    ns = t // tc
    w = HG_WIDTH
    fwd = lambda col: pl.BlockSpec((1, tc, w), lambda i, j: (i, j, col))
    bwd = lambda col: pl.BlockSpec((1, tc, w), lambda i, j: (i, ns - 1 - j, col))
    in_specs = [fwd(2), fwd(3), fwd(5), bwd(2), bwd(4), bwd(5),
                pl.BlockSpec(lb_raw.shape, lambda i, j: (0, 0, 0))]
    args = [u, u, u, u, u, u, lb_raw]
    state_spec = pl.BlockSpec((1, 2, HG_HEADS, HG_KDIM, HG_VDIM), lambda i, j: (i, 0, 0, 0, 0))
    if s0 is not None:
        in_specs.append(state_spec)
        args.append(s0)
    out_specs = [pl.BlockSpec((1, tc, w), lambda i, j: (i, j, 0)),
                 pl.BlockSpec((1, tc, w), lambda i, j: (i, ns - 1 - j, 0))]
    out_shape = [jax.ShapeDtypeStruct((b, t, w), F32), jax.ShapeDtypeStruct((b, t, w), F32)]
    if emit_state:
        out_specs.append(state_spec)
        out_shape.append(jax.ShapeDtypeStruct((b, 2, HG_HEADS, HG_KDIM, HG_VDIM), F32))
    return pl.pallas_call(
        functools.partial(_hgrn_kernel, layer=layer, tc=tc, has_s0=s0 is not None, emit_state=emit_state),
        grid=(b, ns),
        in_specs=in_specs,
        out_specs=out_specs,
        out_shape=out_shape,
        scratch_shapes=[pltpu.VMEM((2, HG_HEADS, HG_VDIM, HG_KDIM), F32)],
        compiler_params=_cparams("arbitrary", "arbitrary"),
        name="hgrn_scan",
    )(*args)


def _even_out_kernel(x_ref, a_ref, of_ref, ob_ref, g_ref, ng_ref, w_ref, mod_ref, o_ref):
    o = of_ref[...] + ob_ref[...]
    gs = _silu(g_ref[...])
    parts = []
    for h in range(HG_HEADS):
        sl = slice(h * HG_VDIM, (h + 1) * HG_VDIM)
        oh = o[:, sl]
        ms = jnp.mean(oh * oh, axis=-1, keepdims=True)
        parts.append(oh * lax.rsqrt(ms + EPS) * ng_ref[...] * gs[:, sl])
    r = jnp.concatenate(parts, axis=1).astype(BF16)
    c = CONV_CH
    mix = _dot(a_ref[...].astype(BF16), w_ref[0:c, :]) + _dot(r, w_ref[c:, :])
    o_ref[...] = x_ref[...] + mod_ref[0, 5:6, :] * mix


def _even_out(x, a, o_f, o_b, u, norm_g, w_out, mod):
    n, d = x.shape
    tm = ROW_TILE
    per_group = (n // mod.shape[0]) // tm
    w = HG_WIDTH
    row = lambda width: pl.BlockSpec((tm, width), lambda i: (i, 0))
    return pl.pallas_call(
        _even_out_kernel,
        grid=(n // tm,),
        in_specs=[row(d), row(CONV_CH), row(w), row(w),
                  pl.BlockSpec((tm, w), lambda i: (i, 6)),
                  pl.BlockSpec((1, HG_VDIM), lambda i: (0, 0)),
                  _resident(w_out.shape),
                  pl.BlockSpec((1, N_MOD, d), lambda i: (i // per_group, 0, 0))],
        out_specs=row(d),
        out_shape=jax.ShapeDtypeStruct((n, d), F32),
        compiler_params=_cparams("arbitrary"),
        name="even_out",
    )(x, a, o_f, o_b, u, norm_g.reshape(1, HG_VDIM), w_out, mod)


def _head_norm(x, gain, seg_ones):
    parts = []
    for j in range(x.shape[1] // 256):
        xs = x[:, 256 * j:256 * (j + 1)]
        ss = _dot((xs * xs).astype(BF16), seg_ones)
        parts.append(xs * lax.rsqrt(ss * (1.0 / HEAD_DIM) + EPS))
    y = parts[0] if len(parts) == 1 else jnp.concatenate(parts, axis=1)
    return y * gain


def _rope(x, cos, sin):
    lane = lax.broadcasted_iota(jnp.int32, (1, LANES), 1)
    first = (lane & 31) < 16
    parts = []
    for j in range(x.shape[1] // LANES):
        xs = x[:, LANES * j:LANES * (j + 1)]
        partner = jnp.where(first, pltpu.roll(xs, LANES - 16, 1), pltpu.roll(xs, 16, 1))
        parts.append(xs * cos + partner * sin)
    return jnp.concatenate(parts, axis=1)


def _qkv_kernel(*refs, rope):
    x_ref, mod_ref, g_ref, w_ref, qn_ref, kn_ref = refs[:6]
    pos = 6
    if rope:
        cos_ref, sin_ref = refs[6], refs[7]
        pos = 8
    q_ref, k_ref, v_ref = refs[pos:pos + 3]
    h = _prenorm(x_ref[...], g_ref[...], mod_ref, 1).astype(BF16)
    qkv = _dot(h, w_ref[...])
    nq = N_HEADS * HEAD_DIM
    nk = N_KV_HEADS * HEAD_DIM
    r = lax.broadcasted_iota(jnp.int32, (256, 256), 0)
    c = lax.broadcasted_iota(jnp.int32, (256, 256), 1)
    seg_ones = ((r >> 6) == (c >> 6)).astype(BF16)
    q = _head_norm(qkv[:, :nq], qn_ref[...], seg_ones)
    k = _head_norm(qkv[:, nq:nq + nk], kn_ref[...], seg_ones)
    if rope:
        q = _rope(q, cos_ref[...], sin_ref[...])
        k = _rope(k, cos_ref[...], sin_ref[...])
    q_ref[...] = q * (HEAD_DIM ** -0.5)
    k_ref[...] = k
    v_ref[...] = qkv[:, nq + nk:]


def _qkv(x, mod, g, w, qn, kn, rope_tabs):
    n, d = x.shape
    tm = ROW_TILE
    per_group = (n // mod.shape[0]) // tm
    nq = N_HEADS * HEAD_DIM
    nk = N_KV_HEADS * HEAD_DIM
    in_specs = [pl.BlockSpec((tm, d), lambda i: (i, 0)),
                pl.BlockSpec((1, N_MOD, d), lambda i: (i // per_group, 0, 0)),
                pl.BlockSpec((1, d), lambda i: (0, 0)),
                _resident(w.shape),
                pl.BlockSpec((1, nq), lambda i: (0, 0)),
                pl.BlockSpec((1, nk), lambda i: (0, 0))]
    args = [x, mod, g.reshape(1, d), w, jnp.tile(qn, N_HEADS).reshape(1, nq), jnp.tile(kn, N_KV_HEADS).reshape(1, nk)]
    if rope_tabs is not None:
        cos, sin = rope_tabs
        per_seq = cos.shape[0] // tm
        in_specs += [pl.BlockSpec((tm, LANES), lambda i: (i % per_seq, 0))] * 2
        args += [cos, sin]
    return pl.pallas_call(
        functools.partial(_qkv_kernel, rope=rope_tabs is not None),
        grid=(n // tm,),
        in_specs=in_specs,
        out_specs=[pl.BlockSpec((tm, nq), lambda i: (i, 0)),
                   pl.BlockSpec((tm, nk), lambda i: (i, 0)),
                   pl.BlockSpec((tm, nk), lambda i: (i, 0))],
        out_shape=[jax.ShapeDtypeStruct((n, nq), F32),
                   jax.ShapeDtypeStruct((n, nk), F32),
                   jax.ShapeDtypeStruct((n, nk), F32)],
        compiler_params=_cparams("arbitrary"),
        name="qkv_proj",
    )(*args)


def _rope_tables(t):
    posn = jnp.arange(t)
    row = (posn // GRID_W).astype(F32)
    col = (posn % GRID_W).astype(F32)
    inv = ROPE_BASE ** (-jnp.arange(0, ROPE_AX, 2, dtype=F32) / ROPE_AX)
    ang_r = row[:, None] * inv[None, :]
    ang_c = col[:, None] * inv[None, :]
    cos = jnp.concatenate([jnp.cos(ang_r)] * 2 + [jnp.cos(ang_c)] * 2, axis=1)
    sin = jnp.concatenate([-jnp.sin(ang_r), jnp.sin(ang_r), -jnp.sin(ang_c), jnp.sin(ang_c)], axis=1)
    return jnp.concatenate([cos, cos], axis=1), jnp.concatenate([sin, sin], axis=1)


def _attend(q, kall, vall, mask, sink_ref, tq):
    lane = lax.broadcasted_iota(jnp.int32, (1, LANES), 1)
    low = lane < HEAD_DIM
    slabs = [None] * (N_HEADS // 2)
    for j in range(N_KV_HEADS // 2):
        ks = kall[:, LANES * j:LANES * (j + 1)]
        vs = vall[:, LANES * j:LANES * (j + 1)]
        ks_r = pltpu.roll(ks, HEAD_DIM, 1)
        vs_r = pltpu.roll(vs, HEAD_DIM, 1)
        for e in range(2):
            kv = 2 * j + e
            own = low if e == 0 else jnp.logical_not(low)
            kboth = jnp.where(own, ks, ks_r).astype(BF16)
            vboth = jnp.where(own, vs, vs_r).astype(BF16)
            qs, sk = [], []
            for g in range(GROUP):
                h = GROUP * kv + g
                half = low if h % 2 == 0 else jnp.logical_not(low)
                qs.append(jnp.where(half, q[:, LANES * (h // 2):LANES * (h // 2 + 1)], 0.0).astype(BF16))
                sk.append(jnp.full((tq, 1), sink_ref[h], F32))
            s = _dot_nt(jnp.concatenate(qs, axis=0), kboth)
            if mask is not None:
                s = jnp.where(mask, s, -jnp.inf)
            sink = jnp.concatenate(sk, axis=0)
            m = jnp.maximum(jnp.max(s, axis=-1, keepdims=True), sink)
            p = jnp.exp(s - m)
            den = jnp.sum(p, axis=-1, keepdims=True) + jnp.exp(sink - m)
            o = _dot(p.astype(BF16), vboth) * (1.0 / den)
            slabs[2 * kv] = jnp.where(low, o[0:tq], o[tq:2 * tq])
            slabs[2 * kv + 1] = jnp.where(low, o[2 * tq:3 * tq], o[3 * tq:4 * tq])
    return jnp.concatenate(slabs, axis=1)


def _ctx_attn_kernel(sink_ref, q_ref, k_ref, v_ref, o_ref, *, tq):
    o_ref[...] = _attend(q_ref[...], k_ref[...], v_ref[...], None, sink_ref, tq)


def _ctx_attention(q, k, v, sinks, seq):
    n = q.shape[0]
    nq = q.shape[1]
    nk = k.shape[1]
    return pl.pallas_call(
        functools.partial(_ctx_attn_kernel, tq=seq),
        grid=(n // seq,),
        in_specs=[pl.BlockSpec(memory_space=pltpu.SMEM),
                  pl.BlockSpec((seq, nq), lambda i: (i, 0)),
                  pl.BlockSpec((seq, nk), lambda i: (i, 0)),
                  pl.BlockSpec((seq, nk), lambda i: (i, 0))],
        out_specs=pl.BlockSpec((seq, nq), lambda i: (i, 0)),
        out_shape=jax.ShapeDtypeStruct((n, nq), F32),
        compiler_params=_cparams("arbitrary"),
        name="ctx_attention",
    )(sinks, q, k, v)


def _lat_attn_kernel(sink_ref, q_ref, ck_ref, cv_ref, kp_ref, kc_ref, kn_ref, vp_ref, vc_ref, vn_ref, o_ref, *, tq):
    n = pl.program_id(1)
    nb = pl.num_programs(1)
    kall = jnp.concatenate([ck_ref[0], kp_ref[...], kc_ref[...], kn_ref[...]], axis=0)
    vall = jnp.concatenate([cv_ref[0], vp_ref[...], vc_ref[...], vn_ref[...]], axis=0)
    nctx = ck_ref.shape[1]
    nk = nctx + 3 * tq
    qi = lax.broadcasted_iota(jnp.int32, (GROUP * tq, nk), 0) & (tq - 1)
    col = lax.broadcasted_iota(jnp.int32, (GROUP * tq, nk), 1)
    prev_ok = (col >= nctx) & (col < nctx + tq) & (col - nctx >= qi) & (n > 0)
    cur_ok = (col >= nctx + tq) & (col < nctx + 2 * tq)
    next_ok = (col >= nctx + 2 * tq) & (col - (nctx + 2 * tq) <= qi) & (n < nb - 1)
    mask = (col < nctx) | prev_ok | cur_ok | next_ok
    o_ref[...] = _attend(q_ref[...], kall, vall, mask, sink_ref, tq)


def _lat_attention(q, k, v, ck, cv, sinks, seq):
    n = q.shape[0]
    nq = q.shape[1]
    nk = k.shape[1]
    tq = ATT_BLOCK
    nb = seq // tq
    nctx = ck.shape[1]
    blk = lambda shift: pl.BlockSpec((tq, nk), lambda b, j: (b * nb + jnp.clip(j + shift, 0, nb - 1), 0))
    ctx = pl.BlockSpec((1, nctx, nk), lambda b, j: (b, 0, 0))
    return pl.pallas_call(
        functools.partial(_lat_attn_kernel, tq=tq),
        grid=(n // seq, nb),
        in_specs=[pl.BlockSpec(memory_space=pltpu.SMEM),
                  pl.BlockSpec((tq, nq), lambda b, j: (b * nb + j, 0)),
                  ctx, ctx, blk(-1), blk(0), blk(1), blk(-1), blk(0), blk(1)],
        out_specs=pl.BlockSpec((tq, nq), lambda b, j: (b * nb + j, 0)),
        out_shape=jax.ShapeDtypeStruct((n, nq), F32),
        compiler_params=_cparams("arbitrary", "arbitrary"),
        name="lat_attention",
    )(sinks, q, ck, cv, k, k, k, v, v, v)


def _attn_out_kernel(x_ref, a_ref, w_ref, mod_ref, o_ref):
    o_ref[...] = x_ref[...] + mod_ref[0, 5:6, :] * _dot(a_ref[...].astype(BF16), w_ref[...])


def _attn_out(x, a, w, mod):
    n, d = x.shape
    tm = ROW_TILE
    per_group = (n // mod.shape[0]) // tm
    return pl.pallas_call(
        _attn_out_kernel,
        grid=(n // tm,),
        in_specs=[pl.BlockSpec((tm, d), lambda i: (i, 0)),
                  pl.BlockSpec((tm, a.shape[1]), lambda i: (i, 0)),
                  _resident(w.shape),
                  pl.BlockSpec((1, N_MOD, d), lambda i: (i // per_group, 0, 0))],
        out_specs=pl.BlockSpec((tm, d), lambda i: (i, 0)),
        out_shape=jax.ShapeDtypeStruct((n, d), F32),
        compiler_params=_cparams("arbitrary"),
        name="attn_out",
    )(x, a, w, mod)


def kernel(x_prompt, x_sample, c, state_hgrn, cache_k, cache_v, c_ctx, w_mod, b_mod, norm_g, ffn_w_in, ffn_w_out, ev_w_in, ev_w_out, conv_w, conv_b, conv_ln_g, conv_ln_b, hg_lb_raw, hg_norm_g, od_w_in, od_w_out, q_norm_g, k_norm_g, sinks):
    batch, seq, d = x_prompt.shape
    dec_batch, dec_seq, _ = x_sample.shape
    depth = w_mod.shape[0]
    ffn_in = ffn_w_in.astype(BF16)
    ffn_out = ffn_w_out.astype(BF16)
    ev_in = ev_w_in.astype(BF16)
    ev_out = ev_w_out.astype(BF16)
    od_in = od_w_in.astype(BF16)
    od_out = od_w_out.astype(BF16)

    cond = jnp.concatenate([c_ctx[None, :], c, jnp.zeros((8 - 1 - dec_batch, d), F32)], axis=0)
    mod = _adaln(cond, w_mod, b_mod).reshape(depth, 8, N_MOD, d)
    rope_tabs = _rope_tables(dec_seq)

    streams = [
        dict(x=x_prompt.reshape(batch * seq, d), nb=batch, t=seq, lo=0, hi=1, latent=False),
        dict(x=x_sample.reshape(dec_batch * dec_seq, d), nb=dec_batch, t=dec_seq, lo=1, hi=1 + dec_batch, latent=True),
    ]
    hg_states, ks_new, vs_new = [], [], []
    for st in streams:
        x = st["x"]
        nb, t, latent = st["nb"], st["t"], st["latent"]
        for l in range(depth):
            m = mod[l, st["lo"]:st["hi"]]
            x = _ffn(x, m, norm_g[l, 0], ffn_in[l, 0], ffn_out[l, 0], 0)
            if l % 2 == 0:
                e = l // 2
                u = _proj(x, m, norm_g[l, 1], ev_in[e]).reshape(nb, t, EVEN_IN)
                a = _conv_module(u, conv_w[e], conv_b[e], conv_ln_g[e], conv_ln_b[e])
                s0 = state_hgrn[:, e] if latent else None
                res = _hgrn_scan(u, hg_lb_raw, s0, l, emit_state=not latent)
                if not latent:
                    hg_states.append(res[2])
                n = nb * t
                x = _even_out(x, a.reshape(n, CONV_CH), res[0].reshape(n, HG_WIDTH), res[1].reshape(n, HG_WIDTH),
                              u.reshape(n, EVEN_IN), hg_norm_g[e], ev_out[e], m)
            else:
                o = l // 2
                q, k, v = _qkv(x, m, norm_g[l, 1], od_in[o], q_norm_g[o], k_norm_g[o], rope_tabs if latent else None)
                if latent:
                    nk = N_KV_HEADS * HEAD_DIM
                    ck = cache_k[:, o].reshape(nb, -1, nk)
                    cv = cache_v[:, o].reshape(nb, -1, nk)
                    att = _lat_attention(q, k, v, ck, cv, sinks[o], t)
                else:
                    att = _ctx_attention(q, k, v, sinks[o], t)
                    ks_new.append(k.reshape(nb, t, N_KV_HEADS, HEAD_DIM))
                    vs_new.append(v.reshape(nb, t, N_KV_HEADS, HEAD_DIM))
                x = _attn_out(x, att, od_out[o], m)
            x = _ffn(x, m, norm_g[l, 2], ffn_in[l, 1], ffn_out[l, 1], 2)
        st["y"] = x

    y_prompt = streams[0]["y"].reshape(batch, seq, d)
    y_sample = streams[1]["y"].reshape(dec_batch, dec_seq, d)
    new_state_hgrn = jnp.stack(hg_states, axis=1)
    new_cache_k = jnp.stack(ks_new, axis=1)
    new_cache_v = jnp.stack(vs_new, axis=1)
    return (y_prompt, y_sample, new_state_hgrn, new_cache_k, new_cache_v)
```

```python
import functools

import jax
import jax.numpy as jnp
from jax import lax
from jax.experimental import pallas as pl
from jax.experimental.pallas import tpu as pltpu

F32 = jnp.float32
BF16 = jnp.bfloat16

D_MODEL = 1024
DEPTH = 2
N_MOD = 9
D_FF = 2816
EPS = 1e-6
LN_EPS = 1e-5
LOG2_E = 1.4426950408889634
CONV_CH = 512
CONV_W = 31
CONV_HALO = 16
HG_HEADS = 4
HG_KDIM = 128
HG_VDIM = 128
HG_WIDTH = HG_HEADS * HG_KDIM
HG_CHUNK = 64
HG_SUB = 8
HG_TILE = 256
HG_SAFE_LOG2 = 96.0
EVEN_IN = 2 * CONV_CH + 5 * HG_WIDTH
N_HEADS = 16
N_KV_HEADS = 4
HEAD_DIM = 64
GROUP = N_HEADS // N_KV_HEADS
ATT_BLOCK = 128
GRID_W = 64
ROPE_AX = HEAD_DIM // 2
ROPE_BASE = 10000.0
LANES = 128
VMEM_LIMIT_BYTES = 56 * 1024 * 1024
ROW_TILE = 512


def _cparams(*sem):
    return pltpu.CompilerParams(dimension_semantics=sem, vmem_limit_bytes=VMEM_LIMIT_BYTES)


def _dot(a, b):
    return jnp.dot(a, b, preferred_element_type=F32)


def _dot_nt(a, b):
    return lax.dot_general(a, b, (((1,), (1,)), ((), ())), preferred_element_type=F32)


def _dot_tn(a, b):
    return lax.dot_general(a, b, (((0,), (0,)), ((), ())), preferred_element_type=F32)


def _silu(x):
    return x * jax.nn.sigmoid(x)


def _resident(shape):
    return pl.BlockSpec(shape, lambda *_: (0,) * len(shape), pipeline_mode=pl.Buffered(1))


def _prenorm(x, g, mod_ref, slot):
    shift = mod_ref[0, 3 * slot:3 * slot + 1, :]
    scale = mod_ref[0, 3 * slot + 1:3 * slot + 2, :]
    ms = jnp.mean(x * x, axis=-1, keepdims=True)
    y = x * lax.rsqrt(ms + EPS) * g
    return y * (1.0 + scale) + shift


def _mod_kernel(c_ref, w_ref, b_ref, o_ref):
    c = c_ref[...]
    o_ref[0] = _dot(_silu(c).astype(BF16), w_ref[0].astype(BF16)) + b_ref[0]


def _adaln(cond, w_mod, b_mod):
    n_layers, d, n = w_mod.shape
    r = cond.shape[0]
    tn = 1024
    return pl.pallas_call(
        _mod_kernel,
        grid=(n_layers, n // tn),
        in_specs=[pl.BlockSpec((r, d), lambda l, j: (0, 0)),
                  pl.BlockSpec((1, d, tn), lambda l, j: (l, 0, j)),
                  pl.BlockSpec((1, 1, tn), lambda l, j: (l, 0, j))],
        out_specs=pl.BlockSpec((1, r, tn), lambda l, j: (l, 0, j)),
        out_shape=jax.ShapeDtypeStruct((n_layers, r, n), F32),
        compiler_params=_cparams("arbitrary", "arbitrary"),
        name="adaln",
    )(cond, w_mod, b_mod.reshape(n_layers, 1, n))


def _ffn_chunks():
    out, c0 = [], 0
    while c0 < D_FF:
        cw = min(1024, D_FF - c0)
        out.append((c0, cw))
        c0 += cw
    return out


def _ffn_kernel(x_ref, mod_ref, g_ref, win_ref, wout_ref, o_ref, act_ref, *, slot):
    x = x_ref[...]
    h = _prenorm(x, g_ref[...], mod_ref, slot).astype(BF16)
    for c0, cw in _ffn_chunks():
        gt = _dot(h, win_ref[:, c0:c0 + cw])
        up = _dot(h, win_ref[:, D_FF + c0:D_FF + c0 + cw])
        act_ref[:, c0:c0 + cw] = (_silu(gt) * up).astype(BF16)
    out = _dot(act_ref[...], wout_ref[...])
    gate = mod_ref[0, 3 * slot + 2:3 * slot + 3, :]
    o_ref[...] = x + (0.5 * gate) * out


def _ffn(x, mod, g, w_in, w_out, slot):
    n, d = x.shape
    tm = ROW_TILE
    per_group = (n // mod.shape[0]) // tm
    return pl.pallas_call(
        functools.partial(_ffn_kernel, slot=slot),
        grid=(n // tm,),
        in_specs=[pl.BlockSpec((tm, d), lambda i: (i, 0)),
                  pl.BlockSpec((1, N_MOD, d), lambda i: (i // per_group, 0, 0)),
                  pl.BlockSpec((1, d), lambda i: (0, 0)),
                  _resident(w_in.shape),
                  _resident(w_out.shape)],
        out_specs=pl.BlockSpec((tm, d), lambda i: (i, 0)),
        out_shape=jax.ShapeDtypeStruct((n, d), F32),
        scratch_shapes=[pltpu.VMEM((tm, D_FF), BF16)],
        compiler_params=_cparams("arbitrary"),
        name="ffn",
    )(x, mod, g.reshape(1, d), w_in, w_out)


def _proj_kernel(x_ref, mod_ref, g_ref, w_ref, o_ref):
    h = _prenorm(x_ref[...], g_ref[...], mod_ref, 1).astype(BF16)
    o_ref[...] = _dot(h, w_ref[...])


def _proj(x, mod, g, w):
    n, d = x.shape
    nout = w.shape[1]
    tm = ROW_TILE
    per_group = (n // mod.shape[0]) // tm
    return pl.pallas_call(
        _proj_kernel,
        grid=(n // tm,),
        in_specs=[pl.BlockSpec((tm, d), lambda i: (i, 0)),
                  pl.BlockSpec((1, N_MOD, d), lambda i: (i // per_group, 0, 0)),
                  pl.BlockSpec((1, d), lambda i: (0, 0)),
                  _resident(w.shape)],
        out_specs=pl.BlockSpec((tm, nout), lambda i: (i, 0)),
        out_shape=jax.ShapeDtypeStruct((n, nout), F32),
        compiler_params=_cparams("arbitrary"),
        name="even_proj",
    )(x, mod, g.reshape(1, d), w)


def _conv_kernel(cur_ref, prev_ref, next_ref, w_ref, cb_ref, lg_ref, lb_ref, o_ref, ext_ref, sh_ref, *, tt):
    t = pl.program_id(1)
    nt = pl.num_programs(1)
    c = CONV_CH
    h = CONV_HALO
    sub = 8

    def glu(v):
        return v[:, :c] * jax.nn.sigmoid(v[:, c:])

    ext_ref[0:h, :] = jnp.where(t > 0, glu(prev_ref[0]), 0.0)
    ext_ref[h:h + tt, :] = glu(cur_ref[0])
    ext_ref[h + tt:2 * h + tt, :] = jnp.where(t < nt - 1, glu(next_ref[0]), 0.0)
    span = tt + 2 * h - sub
    for s in range(1, sub):
        sh_ref[s - 1, :, :] = ext_ref[s:s + span, :]
    rs = 64
    first = h - CONV_W // 2
    for r in range(0, tt, rs):
        acc = jnp.zeros((rs, c), F32)
        for j in range(CONV_W):
            s, a0 = (first + j) % sub, r + (first + j) // sub * sub
            rows = ext_ref[a0:a0 + rs, :] if s == 0 else sh_ref[s - 1, a0:a0 + rs, :]
            acc = acc + rows * w_ref[j:j + 1, :]
        y = acc + cb_ref[...]
        mu = jnp.mean(y, axis=-1, keepdims=True)
        dlt = y - mu
        var = jnp.mean(dlt * dlt, axis=-1, keepdims=True)
        yn = dlt * lax.rsqrt(var + LN_EPS) * lg_ref[...] + lb_ref[...]
        o_ref[0, r:r + rs, :] = _silu(yn)


def _conv_module(u, w, cb, ln_g, ln_b):
    b, t, _ = u.shape
    tt = min(t, 512)
    c = CONV_CH
    hb = tt // CONV_HALO
    last = t // CONV_HALO - 1
    wp = jnp.zeros((32, c), F32).at[:CONV_W].set(w)
    return pl.pallas_call(
        functools.partial(_conv_kernel, tt=tt),
        grid=(b, t // tt),
        in_specs=[pl.BlockSpec((1, tt, 2 * c), lambda i, j: (i, j, 0)),
                  pl.BlockSpec((1, CONV_HALO, 2 * c), lambda i, j: (i, jnp.maximum(j * hb - 1, 0), 0)),
                  pl.BlockSpec((1, CONV_HALO, 2 * c), lambda i, j: (i, jnp.minimum((j + 1) * hb, last), 0)),
                  pl.BlockSpec((32, c), lambda i, j: (0, 0)),
                  pl.BlockSpec((1, c), lambda i, j: (0, 0)),
                  pl.BlockSpec((1, c), lambda i, j: (0, 0)),
                  pl.BlockSpec((1, c), lambda i, j: (0, 0))],
        out_specs=pl.BlockSpec((1, tt, c), lambda i, j: (i, j, 0)),
        out_shape=jax.ShapeDtypeStruct((b, t, c), F32),
        scratch_shapes=[pltpu.VMEM((tt + 2 * CONV_HALO, c), F32),
                        pltpu.VMEM((7, tt + 2 * CONV_HALO - 8, c), F32)],
        compiler_params=_cparams("arbitrary", "arbitrary"),
        name="conv_module",
    )(u, u, u, wp, cb.reshape(1, c), ln_g.reshape(1, c), ln_b.reshape(1, c))


def _split3(x):
    hi = x.astype(BF16)
    r1 = x - hi.astype(F32)
    mid = r1.astype(BF16)
    lo = (r1 - mid.astype(F32)).astype(BF16)
    return hi, mid, lo


def _hgrn_scores(q, b_ref, kk_ref, d, base, lanes, rev, wsel, sub_iota, lane_iota):
    c = HG_CHUNK
    sb = HG_SUB
    nb = c // sb
    b = b_ref[d, base:base + c, lanes]
    kk = kk_ref[d, base:base + c, lanes]
    rows = []
    for i in range(nb):
        r0 = i * sb
        lo, hi = (r0 + sb, c) if rev else (0, r0)
        if hi <= lo:
            rows.append(jnp.zeros((sb, c), F32))
            continue
        edge = base + (r0 + sb if rev else r0 - 1)
        ref = b_ref[d, edge:edge + 1, lanes]
        qt = q[r0:r0 + sb, :] * jnp.exp2(b[r0:r0 + sb, :] - ref)
        kt = kk[lo:hi, :] * jnp.exp2(ref - b[lo:hi, :])
        pieces = [jnp.zeros((lo, HG_KDIM), F32), kt] if rev else [kt, jnp.zeros((c - hi, HG_KDIM), F32)]
        rows.append(_dot_nt(qt.astype(BF16), jnp.concatenate(pieces, axis=0).astype(BF16)))
    xs = []
    for i in range(nb):
        r0 = i * sb
        bq = b[r0:r0 + sb, :]
        qq = q[r0:r0 + sb, :]
        for p in range(sb // 2):
            pair = []
            for s in (2 * p, 2 * p + 1):
                keep = (sub_iota <= s) if rev else (sub_iota >= s)
                row = base + r0 + s
                e = jnp.exp2(bq - b_ref[d, row:row + 1, lanes])
                pair.append(jnp.where(keep, qq * kk_ref[d, row:row + 1, lanes] * e, 0.0))
            xs.append(jnp.concatenate(pair, axis=1))
    y = _dot(jnp.concatenate(xs, axis=0).astype(BF16), wsel)
    blocks = []
    for i in range(nb):
        r0 = i * sb
        acc = rows[i]
        for p in range(sb // 2):
            g0 = (i * (sb // 2) + p) * sb
            acc = jnp.where((lane_iota >> 1) == (r0 // 2 + p), y[g0:g0 + sb, :c], acc)
        blocks.append(acc)
    return jnp.concatenate(blocks, axis=0)


def _hgrn_kernel(*refs, layer, tc, has_s0, emit_state):
    qf_ref, zf_ref, vf_ref, qb_ref, zb_ref, vb_ref, lbraw_ref = refs[:7]
    pos = 7
    s0_ref = None
    if has_s0:
        s0_ref = refs[pos]
        pos += 1
    of_ref, ob_ref = refs[pos], refs[pos + 1]
    pos += 2
    sf_ref = None
    if emit_state:
        sf_ref = refs[pos]
        pos += 1
    st_ref, b_ref, kk_ref, qs_ref, qe_ref, kd_ref, v16_ref, dec_ref, sc_ref = refs[pos:pos + 9]

    s = pl.program_id(1)
    ns = pl.num_programs(1)
    c = HG_CHUNK
    nch = tc // c
    w = HG_WIDTH

    @pl.when(s == 0)
    def _init():
        for d in range(2):
            for h in range(HG_HEADS):
                if has_s0:
                    st_ref[d, h] = s0_ref[0, d, h].T
                else:
                    st_ref[d, h] = jnp.zeros((HG_VDIM, HG_KDIM), F32)

    def lower_bound(d):
        rows = [lbraw_ref[d, j:j + 1, :] for j in range(DEPTH + 1)]
        m = functools.reduce(jnp.maximum, rows)
        ex = [jnp.exp(r - m) for r in rows]
        tot = functools.reduce(lambda a, b_: a + b_, ex)
        return functools.reduce(lambda a, b_: a + b_, [e / tot for e in ex[:layer + 1]])

    r64 = lax.broadcasted_iota(jnp.int32, (c, c), 0)
    c64 = lax.broadcasted_iota(jnp.int32, (c, c), 1)
    tri = ((r64 >= c64).astype(BF16), (r64 <= c64).astype(BF16))
    rsel = lax.broadcasted_iota(jnp.int32, (2 * LANES, LANES), 0)
    csel = lax.broadcasted_iota(jnp.int32, (2 * LANES, LANES), 1)
    wsel = ((rsel >> 7) == (csel & 1)).astype(BF16)
    sub_iota = lax.broadcasted_iota(jnp.int32, (HG_SUB, LANES), 0)
    lane_iota = lax.broadcasted_iota(jnp.int32, (HG_SUB, c), 1)

    dirs = ((qf_ref, zf_ref, vf_ref, of_ref), (qb_ref, zb_ref, vb_ref, ob_ref))

    sums = {}
    for d, (q_ref, z_ref, v_ref, _) in enumerate(dirs):
        lb = lower_bound(d)
        for ci in range(nch):
            base = ci * c
            f = lb + (1.0 - lb) * jax.nn.sigmoid(z_ref[0, base:base + c, :])
            kk_ref[d, base:base + c, :] = 1.0 - f
            sums[d, ci] = _dot(tri[d], jnp.concatenate(_split3(jnp.log(f) * LOG2_E), axis=1))
            qs_ref[d, base:base + c, :] = _silu(q_ref[0, base:base + c, :])
            v16_ref[d, base:base + c, :] = v_ref[0, base:base + c, :].astype(BF16)
    b_low = None
    for d in range(2):
        for ci in range(nch):
            base = ci * c
            cs = sums[d, ci]
            b = cs[:, :w] + cs[:, w:2 * w] + cs[:, 2 * w:]
            b_ref[d, base:base + c, :] = b
            b_end = b[0:1, :] if d == 1 else b[c - 1:c, :]
            b_low = b_end if b_low is None else jnp.minimum(b_low, b_end)
            qe_ref[d, base:base + c, :] = (qs_ref[d, base:base + c, :] * jnp.exp2(b)).astype(BF16)
            kd_ref[d, base:base + c, :] = (kk_ref[d, base:base + c, :] * jnp.exp2(b_end - b)).astype(BF16)
            dec_ref[d, ci * HG_SUB:ci * HG_SUB + 1, :] = jnp.exp2(b_end)

    safe = jnp.min(b_low) >= -HG_SAFE_LOG2

    @pl.when(safe)
    def _fast():
        for d in range(2):
            keep = (r64 <= c64) if d == 1 else (r64 >= c64)
            for ci in range(nch):
                rows = slice(ci * c, (ci + 1) * c)
                kinv = (kk_ref[d, rows, :] * jnp.exp2(-b_ref[d, rows, :])).astype(BF16)
                for h in range(HG_HEADS):
                    sl = slice(h * HG_KDIM, (h + 1) * HG_KDIM)
                    sc = _dot_nt(qe_ref[d, rows, sl], kinv[:, sl])
                    sc_ref[d, ci, h] = jnp.where(keep, sc, 0.0).astype(BF16)

    @pl.when(jnp.logical_not(safe))
    def _exact():
        for d in range(2):
            for ci in range(nch):
                base = ci * c
                for h in range(HG_HEADS):
                    sl = slice(h * HG_KDIM, (h + 1) * HG_KDIM)
                    sc = _hgrn_scores(qs_ref[d, base:base + c, sl], b_ref, kk_ref, d, base, sl, d == 1,
                                      wsel, sub_iota, lane_iota)
                    sc_ref[d, ci, h] = sc.astype(BF16)

    chains = [(d, h) for d in range(2) for h in range(HG_HEADS)]
    states = [st_ref[d, h] for d, h in chains]
    for step in range(nch):
        for n, (d, h) in enumerate(chains):
            ci = nch - 1 - step if d == 1 else step
            sl = slice(h * HG_KDIM, (h + 1) * HG_KDIM)
            rows = slice(ci * c, (ci + 1) * c)
            v16 = v16_ref[d, rows, sl]
            st = states[n]
            dirs[d][3][0, rows, sl] = (_dot_nt(qe_ref[d, rows, sl], st.astype(BF16))
                                       + _dot(sc_ref[d, ci, h], v16))
            states[n] = st * dec_ref[d, ci * HG_SUB:ci * HG_SUB + 1, sl] + _dot_tn(v16, kd_ref[d, rows, sl])
    for n, (d, h) in enumerate(chains):
        st_ref[d, h] = states[n]

    if emit_state:
        @pl.when(s == ns - 1)
        def _fin():
            for d in range(2):
                for h in range(HG_HEADS):
                    sf_ref[0, d, h] = st_ref[d, h].T


def _hgrn_scan(u, lb_raw, s0, layer, emit_state):
    b, t, _ = u.shape
    tc = HG_TILE
    ns = t // tc
    nch = tc // HG_CHUNK
    w = HG_WIDTH
    fwd = lambda col: pl.BlockSpec((1, tc, w), lambda i, j: (i, j, col))
    bwd = lambda col: pl.BlockSpec((1, tc, w), lambda i, j: (i, ns - 1 - j, col))
    in_specs = [fwd(2), fwd(3), fwd(5), bwd(2), bwd(4), bwd(5),
                pl.BlockSpec(lb_raw.shape, lambda i, j: (0, 0, 0))]
    args = [u, u, u, u, u, u, lb_raw]
    state_spec = pl.BlockSpec((1, 2, HG_HEADS, HG_KDIM, HG_VDIM), lambda i, j: (i, 0, 0, 0, 0))
    if s0 is not None:
        in_specs.append(state_spec)
        args.append(s0)
    out_specs = [pl.BlockSpec((1, tc, w), lambda i, j: (i, j, 0)),
                 pl.BlockSpec((1, tc, w), lambda i, j: (i, ns - 1 - j, 0))]
    out_shape = [jax.ShapeDtypeStruct((b, t, w), F32), jax.ShapeDtypeStruct((b, t, w), F32)]
    if emit_state:
        out_specs.append(state_spec)
        out_shape.append(jax.ShapeDtypeStruct((b, 2, HG_HEADS, HG_KDIM, HG_VDIM), F32))
    return pl.pallas_call(
        functools.partial(_hgrn_kernel, layer=layer, tc=tc, has_s0=s0 is not None, emit_state=emit_state),
        grid=(b, ns),
        in_specs=in_specs,
        out_specs=out_specs,
        out_shape=out_shape,
        scratch_shapes=[pltpu.VMEM((2, HG_HEADS, HG_VDIM, HG_KDIM), F32),
                        pltpu.VMEM((2, tc, w), F32),
                        pltpu.VMEM((2, tc, w), F32),
                        pltpu.VMEM((2, tc, w), F32),
                        pltpu.VMEM((2, tc, w), BF16),
                        pltpu.VMEM((2, tc, w), BF16),
                        pltpu.VMEM((2, tc, w), BF16),
                        pltpu.VMEM((2, nch * HG_SUB, w), F32),
                        pltpu.VMEM((2, nch, HG_HEADS, HG_CHUNK, HG_CHUNK), BF16)],
        compiler_params=_cparams("arbitrary", "arbitrary"),
        name="hgrn_scan",
    )(*args)


def _even_out_kernel(x_ref, a_ref, of_ref, ob_ref, g_ref, ng_ref, w_ref, mod_ref, o_ref):
    o = of_ref[...] + ob_ref[...]
    gs = _silu(g_ref[...])
    parts = []
    for h in range(HG_HEADS):
        sl = slice(h * HG_VDIM, (h + 1) * HG_VDIM)
        oh = o[:, sl]
        ms = jnp.mean(oh * oh, axis=-1, keepdims=True)
        parts.append(oh * lax.rsqrt(ms + EPS) * ng_ref[...] * gs[:, sl])
    r = jnp.concatenate(parts, axis=1).astype(BF16)
    c = CONV_CH
    mix = _dot(a_ref[...].astype(BF16), w_ref[0:c, :]) + _dot(r, w_ref[c:, :])
    o_ref[...] = x_ref[...] + mod_ref[0, 5:6, :] * mix


def _even_out(x, a, o_f, o_b, u, norm_g, w_out, mod):
    n, d = x.shape
    tm = ROW_TILE
    per_group = (n // mod.shape[0]) // tm
    w = HG_WIDTH
    row = lambda width: pl.BlockSpec((tm, width), lambda i: (i, 0))
    return pl.pallas_call(
        _even_out_kernel,
        grid=(n // tm,),
        in_specs=[row(d), row(CONV_CH), row(w), row(w),
                  pl.BlockSpec((tm, w), lambda i: (i, 6)),
                  pl.BlockSpec((1, HG_VDIM), lambda i: (0, 0)),
                  _resident(w_out.shape),
                  pl.BlockSpec((1, N_MOD, d), lambda i: (i // per_group, 0, 0))],
        out_specs=row(d),
        out_shape=jax.ShapeDtypeStruct((n, d), F32),
        compiler_params=_cparams("arbitrary"),
        name="even_out",
    )(x, a, o_f, o_b, u, norm_g.reshape(1, HG_VDIM), w_out, mod)


def _head_norm(x, gain, seg_ones):
    parts = []
    for j in range(x.shape[1] // 256):
        xs = x[:, 256 * j:256 * (j + 1)]
        ss = _dot((xs * xs).astype(BF16), seg_ones)
        parts.append(xs * lax.rsqrt(ss * (1.0 / HEAD_DIM) + EPS))
    y = parts[0] if len(parts) == 1 else jnp.concatenate(parts, axis=1)
    return y * gain


def _rope(x, cos, sin):
    lane = lax.broadcasted_iota(jnp.int32, (1, LANES), 1)
    first = (lane & 31) < 16
    parts = []
    for j in range(x.shape[1] // LANES):
        xs = x[:, LANES * j:LANES * (j + 1)]
        partner = jnp.where(first, pltpu.roll(xs, LANES - 16, 1), pltpu.roll(xs, 16, 1))
        parts.append(xs * cos + partner * sin)
    return jnp.concatenate(parts, axis=1)


def _qkv_kernel(*refs, rope):
    x_ref, mod_ref, g_ref, w_ref, qn_ref, kn_ref = refs[:6]
    pos = 6
    if rope:
        cos_ref, sin_ref = refs[6], refs[7]
        pos = 8
    q_ref, k_ref, v_ref = refs[pos:pos + 3]
    h = _prenorm(x_ref[...], g_ref[...], mod_ref, 1).astype(BF16)
    qkv = _dot(h, w_ref[...])
    nq = N_HEADS * HEAD_DIM
    nk = N_KV_HEADS * HEAD_DIM
    r = lax.broadcasted_iota(jnp.int32, (256, 256), 0)
    c = lax.broadcasted_iota(jnp.int32, (256, 256), 1)
    seg_ones = ((r >> 6) == (c >> 6)).astype(BF16)
    q = _head_norm(qkv[:, :nq], qn_ref[...], seg_ones)
    k = _head_norm(qkv[:, nq:nq + nk], kn_ref[...], seg_ones)
    if rope:
        q = _rope(q, cos_ref[...], sin_ref[...])
        k = _rope(k, cos_ref[...], sin_ref[...])
    q_ref[...] = q * (HEAD_DIM ** -0.5 * LOG2_E)
    k_ref[...] = k
    v_ref[...] = qkv[:, nq + nk:]


def _qkv(x, mod, g, w, qn, kn, rope_tabs):
    n, d = x.shape
    tm = ROW_TILE
    per_group = (n // mod.shape[0]) // tm
    nq = N_HEADS * HEAD_DIM
    nk = N_KV_HEADS * HEAD_DIM
    in_specs = [pl.BlockSpec((tm, d), lambda i: (i, 0)),
                pl.BlockSpec((1, N_MOD, d), lambda i: (i // per_group, 0, 0)),
                pl.BlockSpec((1, d), lambda i: (0, 0)),
                _resident(w.shape),
                pl.BlockSpec((1, nq), lambda i: (0, 0)),
                pl.BlockSpec((1, nk), lambda i: (0, 0))]
    args = [x, mod, g.reshape(1, d), w, jnp.tile(qn, N_HEADS).reshape(1, nq), jnp.tile(kn, N_KV_HEADS).reshape(1, nk)]
    if rope_tabs is not None:
        cos, sin = rope_tabs
        per_seq = cos.shape[0] // tm
        in_specs += [pl.BlockSpec((tm, LANES), lambda i: (i % per_seq, 0))] * 2
        args += [cos, sin]
    return pl.pallas_call(
        functools.partial(_qkv_kernel, rope=rope_tabs is not None),
        grid=(n // tm,),
        in_specs=in_specs,
        out_specs=[pl.BlockSpec((tm, nq), lambda i: (i, 0)),
                   pl.BlockSpec((tm, nk), lambda i: (i, 0)),
                   pl.BlockSpec((tm, nk), lambda i: (i, 0))],
        out_shape=[jax.ShapeDtypeStruct((n, nq), F32),
                   jax.ShapeDtypeStruct((n, nk), F32),
                   jax.ShapeDtypeStruct((n, nk), F32)],
        compiler_params=_cparams("arbitrary"),
        name="qkv_proj",
    )(*args)


def _rope_tables(t):
    posn = jnp.arange(t)
    row = (posn // GRID_W).astype(F32)
    col = (posn % GRID_W).astype(F32)
    inv = ROPE_BASE ** (-jnp.arange(0, ROPE_AX, 2, dtype=F32) / ROPE_AX)
    ang_r = row[:, None] * inv[None, :]
    ang_c = col[:, None] * inv[None, :]
    cos = jnp.concatenate([jnp.cos(ang_r)] * 2 + [jnp.cos(ang_c)] * 2, axis=1)
    sin = jnp.concatenate([-jnp.sin(ang_r), jnp.sin(ang_r), -jnp.sin(ang_c), jnp.sin(ang_c)], axis=1)
    return jnp.concatenate([cos, cos], axis=1), jnp.concatenate([sin, sin], axis=1)


def _attend(q, sections, sink_ref, tq):
    lane = lax.broadcasted_iota(jnp.int32, (1, LANES), 1)
    low = lane < HEAD_DIM
    kall = jnp.concatenate([sec[0] for sec in sections], axis=0)
    vall = jnp.concatenate([sec[1] for sec in sections], axis=0)
    ones = jnp.ones((kall.shape[0], LANES), BF16)
    slabs = [None] * (N_HEADS // 2)
    for j in range(N_KV_HEADS // 2):
        ks = kall[:, LANES * j:LANES * (j + 1)]
        vs = vall[:, LANES * j:LANES * (j + 1)]
        ks_r = pltpu.roll(ks, HEAD_DIM, 1)
        vs_r = pltpu.roll(vs, HEAD_DIM, 1)
        for e in range(2):
            kv = 2 * j + e
            own = low if e == 0 else jnp.logical_not(low)
            kboth = jnp.where(own, ks, ks_r).astype(BF16)
            vext = jnp.concatenate([jnp.where(own, vs, vs_r).astype(BF16), ones], axis=1)
            qs, sk = [], []
            for g in range(GROUP):
                h = GROUP * kv + g
                half = low if h % 2 == 0 else jnp.logical_not(low)
                qs.append(jnp.where(half, q[:, LANES * (h // 2):LANES * (h // 2 + 1)], 0.0).astype(BF16))
                sk.append(jnp.full((tq, LANES), sink_ref[h] * LOG2_E, F32))
            s = _dot_nt(jnp.concatenate(qs, axis=0), kboth)
            pieces, c0 = [], 0
            for k_sec, _, bias in sections:
                for c1 in range(c0, c0 + k_sec.shape[0], LANES):
                    piece = s[:, c1:c1 + LANES]
                    if bias is not None:
                        piece = piece + jnp.concatenate([bias[:, c1 - c0:c1 - c0 + LANES]] * GROUP, axis=0)
                    pieces.append(piece)
                c0 += k_sec.shape[0]
            sink = jnp.concatenate(sk, axis=0)
            top = functools.reduce(jnp.maximum, pieces)
            m = jnp.maximum(jnp.broadcast_to(jnp.max(top, axis=-1, keepdims=True), top.shape), sink)
            p = jnp.concatenate([jnp.exp2(piece - m).astype(BF16) for piece in pieces], axis=1)
            oe = _dot(p, vext)
            o = oe[:, :LANES] * (1.0 / (oe[:, LANES:] + jnp.exp2(sink - m)))
            slabs[2 * kv] = jnp.where(low, o[0:tq], o[tq:2 * tq])
            slabs[2 * kv + 1] = jnp.where(low, o[2 * tq:3 * tq], o[3 * tq:4 * tq])
    return jnp.concatenate(slabs, axis=1)


def _ctx_attn_kernel(sink_ref, q_ref, k_ref, v_ref, o_ref, *, tq):
    o_ref[...] = _attend(q_ref[...], [(k_ref[...], v_ref[...], None)], sink_ref, tq)


def _ctx_attention(q, k, v, sinks, seq):
    n = q.shape[0]
    nq = q.shape[1]
    nk = k.shape[1]
    return pl.pallas_call(
        functools.partial(_ctx_attn_kernel, tq=seq),
        grid=(n // seq,),
        in_specs=[pl.BlockSpec(memory_space=pltpu.SMEM),
                  pl.BlockSpec((seq, nq), lambda i: (i, 0)),
                  pl.BlockSpec((seq, nk), lambda i: (i, 0)),
                  pl.BlockSpec((seq, nk), lambda i: (i, 0))],
        out_specs=pl.BlockSpec((seq, nq), lambda i: (i, 0)),
        out_shape=jax.ShapeDtypeStruct((n, nq), F32),
        compiler_params=_cparams("arbitrary"),
        name="ctx_attention",
    )(sinks, q, k, v)


def _lat_attn_kernel(sink_ref, q_ref, ck_ref, cv_ref, kp_ref, kc_ref, kn_ref, vp_ref, vc_ref, vn_ref, o_ref, *, tq):
    n = pl.program_id(1)
    nb = pl.num_programs(1)
    qi = lax.broadcasted_iota(jnp.int32, (tq, tq), 0)
    kj = lax.broadcasted_iota(jnp.int32, (tq, tq), 1)
    bias_prev = jnp.where((kj >= qi) & (n > 0), 0.0, -jnp.inf)
    bias_next = jnp.where((kj <= qi) & (n < nb - 1), 0.0, -jnp.inf)
    sections = [(ck_ref[0], cv_ref[0], None), (kp_ref[...], vp_ref[...], bias_prev),
                (kc_ref[...], vc_ref[...], None), (kn_ref[...], vn_ref[...], bias_next)]
    o_ref[...] = _attend(q_ref[...], sections, sink_ref, tq)


def _lat_attention(q, k, v, ck, cv, sinks, seq):
    n = q.shape[0]
    nq = q.shape[1]
    nk = k.shape[1]
    tq = ATT_BLOCK
    nb = seq // tq
    nctx = ck.shape[1]
    blk = lambda shift: pl.BlockSpec((tq, nk), lambda b, j: (b * nb + jnp.clip(j + shift, 0, nb - 1), 0))
    ctx = pl.BlockSpec((1, nctx, nk), lambda b, j: (b, 0, 0))
    return pl.pallas_call(
        functools.partial(_lat_attn_kernel, tq=tq),
        grid=(n // seq, nb),
        in_specs=[pl.BlockSpec(memory_space=pltpu.SMEM),
                  pl.BlockSpec((tq, nq), lambda b, j: (b * nb + j, 0)),
                  ctx, ctx, blk(-1), blk(0), blk(1), blk(-1), blk(0), blk(1)],
        out_specs=pl.BlockSpec((tq, nq), lambda b, j: (b * nb + j, 0)),
        out_shape=jax.ShapeDtypeStruct((n, nq), F32),
        compiler_params=_cparams("arbitrary", "arbitrary"),
        name="lat_attention",
    )(sinks, q, ck, cv, k, k, k, v, v, v)


def _attn_out_kernel(x_ref, a_ref, w_ref, mod_ref, o_ref):
    o_ref[...] = x_ref[...] + mod_ref[0, 5:6, :] * _dot(a_ref[...].astype(BF16), w_ref[...])


def _attn_out(x, a, w, mod):
    n, d = x.shape
    tm = ROW_TILE
    per_group = (n // mod.shape[0]) // tm
    return pl.pallas_call(
        _attn_out_kernel,
        grid=(n // tm,),
        in_specs=[pl.BlockSpec((tm, d), lambda i: (i, 0)),
                  pl.BlockSpec((tm, a.shape[1]), lambda i: (i, 0)),
                  _resident(w.shape),
                  pl.BlockSpec((1, N_MOD, d), lambda i: (i // per_group, 0, 0))],
        out_specs=pl.BlockSpec((tm, d), lambda i: (i, 0)),
        out_shape=jax.ShapeDtypeStruct((n, d), F32),
        compiler_params=_cparams("arbitrary"),
        name="attn_out",
    )(x, a, w, mod)


def kernel(x_prompt, x_sample, c, state_hgrn, cache_k, cache_v, c_ctx, w_mod, b_mod, norm_g, ffn_w_in, ffn_w_out, ev_w_in, ev_w_out, conv_w, conv_b, conv_ln_g, conv_ln_b, hg_lb_raw, hg_norm_g, od_w_in, od_w_out, q_norm_g, k_norm_g, sinks):
    batch, seq, d = x_prompt.shape
    dec_batch, dec_seq, _ = x_sample.shape
    depth = w_mod.shape[0]
    ffn_in = ffn_w_in.astype(BF16)
    ffn_out = ffn_w_out.astype(BF16)
    ev_in = ev_w_in.astype(BF16)
    ev_out = ev_w_out.astype(BF16)
    od_in = od_w_in.astype(BF16)
    od_out = od_w_out.astype(BF16)

    cond = jnp.concatenate([c_ctx[None, :], c, jnp.zeros((8 - 1 - dec_batch, d), F32)], axis=0)
    mod = _adaln(cond, w_mod, b_mod).reshape(depth, 8, N_MOD, d)
    rope_tabs = _rope_tables(dec_seq)

    streams = [
        dict(x=x_prompt.reshape(batch * seq, d), nb=batch, t=seq, lo=0, hi=1, latent=False),
        dict(x=x_sample.reshape(dec_batch * dec_seq, d), nb=dec_batch, t=dec_seq, lo=1, hi=1 + dec_batch, latent=True),
    ]
    hg_states, ks_new, vs_new = [], [], []
    for st in streams:
        x = st["x"]
        nb, t, latent = st["nb"], st["t"], st["latent"]
        for l in range(depth):
            m = mod[l, st["lo"]:st["hi"]]
            x = _ffn(x, m, norm_g[l, 0], ffn_in[l, 0], ffn_out[l, 0], 0)
            if l % 2 == 0:
                e = l // 2
                u = _proj(x, m, norm_g[l, 1], ev_in[e]).reshape(nb, t, EVEN_IN)
                a = _conv_module(u, conv_w[e], conv_b[e], conv_ln_g[e], conv_ln_b[e])
                s0 = state_hgrn[:, e] if latent else None
                res = _hgrn_scan(u, hg_lb_raw, s0, l, emit_state=not latent)
                if not latent:
                    hg_states.append(res[2])
                n = nb * t
                x = _even_out(x, a.reshape(n, CONV_CH), res[0].reshape(n, HG_WIDTH), res[1].reshape(n, HG_WIDTH),
                              u.reshape(n, EVEN_IN), hg_norm_g[e], ev_out[e], m)
            else:
                o = l // 2
                q, k, v = _qkv(x, m, norm_g[l, 1], od_in[o], q_norm_g[o], k_norm_g[o], rope_tabs if latent else None)
                if latent:
                    nk = N_KV_HEADS * HEAD_DIM
                    ck = cache_k[:, o].reshape(nb, -1, nk)
                    cv = cache_v[:, o].reshape(nb, -1, nk)
                    att = _lat_attention(q, k, v, ck, cv, sinks[o], t)
                else:
                    att = _ctx_attention(q, k, v, sinks[o], t)
                    ks_new.append(k.reshape(nb, t, N_KV_HEADS, HEAD_DIM))
                    vs_new.append(v.reshape(nb, t, N_KV_HEADS, HEAD_DIM))
                x = _attn_out(x, att, od_out[o], m)
            x = _ffn(x, m, norm_g[l, 2], ffn_in[l, 1], ffn_out[l, 1], 2)
        st["y"] = x

    y_prompt = streams[0]["y"].reshape(batch, seq, d)
    y_sample = streams[1]["y"].reshape(dec_batch, dec_seq, d)
    new_state_hgrn = jnp.stack(hg_states, axis=1)
    new_cache_k = jnp.stack(ks_new, axis=1)
    new_cache_v = jnp.stack(vs_new, axis=1)
    return (y_prompt, y_sample, new_state_hgrn, new_cache_k, new_cache_v)
```

```python
import functools

import jax
import jax.numpy as jnp
from jax import lax
from jax.experimental import pallas as pl
from jax.experimental.pallas import tpu as pltpu

F32 = jnp.float32
BF16 = jnp.bfloat16

D_MODEL = 1024
DEPTH = 2
N_MOD = 9
D_FF = 2816
EPS = 1e-6
LN_EPS = 1e-5
LOG2_E = 1.4426950408889634
CONV_CH = 512
CONV_W = 31
CONV_HALO = 16
HG_HEADS = 4
HG_KDIM = 128
HG_VDIM = 128
HG_WIDTH = HG_HEADS * HG_KDIM
HG_CHUNK = 64
HG_SUB = 8
HG_TILE = 256
HG_SAFE_LOG2 = 96.0
EVEN_IN = 2 * CONV_CH + 5 * HG_WIDTH
N_HEADS = 16
N_KV_HEADS = 4
HEAD_DIM = 64
GROUP = N_HEADS // N_KV_HEADS
ATT_BLOCK = 128
GRID_W = 64
ROPE_AX = HEAD_DIM // 2
ROPE_BASE = 10000.0
LANES = 128
VMEM_LIMIT_BYTES = 56 * 1024 * 1024
ROW_TILE = 512


def _cparams(*sem):
    return pltpu.CompilerParams(dimension_semantics=sem, vmem_limit_bytes=VMEM_LIMIT_BYTES)


def _dot(a, b):
    return jnp.dot(a, b, preferred_element_type=F32)


def _dot_nt(a, b):
    return lax.dot_general(a, b, (((1,), (1,)), ((), ())), preferred_element_type=F32)


def _dot_tn(a, b):
    return lax.dot_general(a, b, (((0,), (0,)), ((), ())), preferred_element_type=F32)


def _silu(x):
    return x * jax.nn.sigmoid(x)


def _resident(shape):
    return pl.BlockSpec(shape, lambda *_: (0,) * len(shape), pipeline_mode=pl.Buffered(1))


def _prenorm(x, g, mod_ref, slot):
    shift = mod_ref[0, 3 * slot:3 * slot + 1, :]
    scale = mod_ref[0, 3 * slot + 1:3 * slot + 2, :]
    ms = jnp.mean(x * x, axis=-1, keepdims=True)
    y = x * lax.rsqrt(ms + EPS) * g
    return y * (1.0 + scale) + shift


def _mod_kernel(c_ref, w_ref, b_ref, o_ref):
    c = c_ref[...]
    o_ref[0] = _dot(_silu(c).astype(BF16), w_ref[0].astype(BF16)) + b_ref[0]


def _adaln(cond, w_mod, b_mod):
    n_layers, d, n = w_mod.shape
    r = cond.shape[0]
    tn = 1024
    return pl.pallas_call(
        _mod_kernel,
        grid=(n_layers, n // tn),
        in_specs=[pl.BlockSpec((r, d), lambda l, j: (0, 0)),
                  pl.BlockSpec((1, d, tn), lambda l, j: (l, 0, j)),
                  pl.BlockSpec((1, 1, tn), lambda l, j: (l, 0, j))],
        out_specs=pl.BlockSpec((1, r, tn), lambda l, j: (l, 0, j)),
        out_shape=jax.ShapeDtypeStruct((n_layers, r, n), F32),
        compiler_params=_cparams("arbitrary", "arbitrary"),
        name="adaln",
    )(cond, w_mod, b_mod.reshape(n_layers, 1, n))


def _ffn_chunks():
    out, c0 = [], 0
    while c0 < D_FF:
        cw = min(1024, D_FF - c0)
        out.append((c0, cw))
        c0 += cw
    return out


def _even_mix(a_ref, of_ref, ob_ref, gz_ref, ng_ref, w_ref):
    o = of_ref[...] + ob_ref[...]
    gs = _silu(gz_ref[...])
    parts = []
    for h in range(HG_HEADS):
        sl = slice(h * HG_VDIM, (h + 1) * HG_VDIM)
        oh = o[:, sl]
        ms = jnp.mean(oh * oh, axis=-1, keepdims=True)
        parts.append(oh * lax.rsqrt(ms + EPS) * ng_ref[...] * gs[:, sl])
    r = jnp.concatenate(parts, axis=1).astype(BF16)
    c = CONV_CH
    return _dot(a_ref[...].astype(BF16), w_ref[0:c, :]) + _dot(r, w_ref[c:, :])


def _ffn_kernel(*refs, slot, tail):
    x_ref, mod_ref, g_ref, win_ref, wout_ref = refs[:5]
    o_ref, act_ref = refs[-2:]
    x = x_ref[...]
    if tail == "even":
        x = x + mod_ref[0, 5:6, :] * _even_mix(*refs[5:11])
    elif tail == "attn":
        x = x + mod_ref[0, 5:6, :] * _dot(refs[5][...].astype(BF16), refs[6][...])
    h = _prenorm(x, g_ref[...], mod_ref, slot).astype(BF16)
    for c0, cw in _ffn_chunks():
        gt = _dot(h, win_ref[:, c0:c0 + cw])
        up = _dot(h, win_ref[:, D_FF + c0:D_FF + c0 + cw])
        act_ref[:, c0:c0 + cw] = (_silu(gt) * up).astype(BF16)
    out = _dot(act_ref[...], wout_ref[...])
    gate = mod_ref[0, 3 * slot + 2:3 * slot + 3, :]
    o_ref[...] = x + (0.5 * gate) * out


def _ffn(x, mod, g, w_in, w_out, slot, tail=None, tail_args=()):
    n, d = x.shape
    tm = ROW_TILE
    per_group = (n // mod.shape[0]) // tm
    row = lambda width: pl.BlockSpec((tm, width), lambda i: (i, 0))
    in_specs = [row(d),
                pl.BlockSpec((1, N_MOD, d), lambda i: (i // per_group, 0, 0)),
                pl.BlockSpec((1, d), lambda i: (0, 0)),
                _resident(w_in.shape),
                _resident(w_out.shape)]
    args = [x, mod, g.reshape(1, d), w_in, w_out]
    if tail == "even":
        a, o_f, o_b, u, norm_g, w_mix = tail_args
        in_specs += [row(CONV_CH), row(HG_WIDTH), row(HG_WIDTH),
                     pl.BlockSpec((tm, HG_WIDTH), lambda i: (i, 6)),
                     pl.BlockSpec((1, HG_VDIM), lambda i: (0, 0)),
                     _resident(w_mix.shape)]
        args += [a, o_f, o_b, u, norm_g.reshape(1, HG_VDIM), w_mix]
    elif tail == "attn":
        att, w_mix = tail_args
        in_specs += [row(att.shape[1]), _resident(w_mix.shape)]
        args += [att, w_mix]
    return pl.pallas_call(
        functools.partial(_ffn_kernel, slot=slot, tail=tail),
        grid=(n // tm,),
        in_specs=in_specs,
        out_specs=row(d),
        out_shape=jax.ShapeDtypeStruct((n, d), F32),
        scratch_shapes=[pltpu.VMEM((tm, D_FF), BF16)],
        compiler_params=_cparams("arbitrary"),
        name="ffn" if tail is None else "mix_ffn",
    )(*args)


def _proj_kernel(x_ref, mod_ref, g_ref, w_ref, o_ref):
    h = _prenorm(x_ref[...], g_ref[...], mod_ref, 1).astype(BF16)
    o_ref[...] = _dot(h, w_ref[...])


def _proj(x, mod, g, w):
    n, d = x.shape
    nout = w.shape[1]
    tm = ROW_TILE
    per_group = (n // mod.shape[0]) // tm
    return pl.pallas_call(
        _proj_kernel,
        grid=(n // tm,),
        in_specs=[pl.BlockSpec((tm, d), lambda i: (i, 0)),
                  pl.BlockSpec((1, N_MOD, d), lambda i: (i // per_group, 0, 0)),
                  pl.BlockSpec((1, d), lambda i: (0, 0)),
                  _resident(w.shape)],
        out_specs=pl.BlockSpec((tm, nout), lambda i: (i, 0)),
        out_shape=jax.ShapeDtypeStruct((n, nout), F32),
        compiler_params=_cparams("arbitrary"),
        name="even_proj",
    )(x, mod, g.reshape(1, d), w)


def _conv_kernel(cur_ref, prev_ref, next_ref, w_ref, cb_ref, lg_ref, lb_ref, o_ref, ext_ref, sh_ref, *, tt):
    t = pl.program_id(1)
    nt = pl.num_programs(1)
    c = CONV_CH
    h = CONV_HALO
    sub = 8

    def glu(v):
        return v[:, :c] * jax.nn.sigmoid(v[:, c:])

    ext_ref[0:h, :] = jnp.where(t > 0, glu(prev_ref[0]), 0.0)
    ext_ref[h:h + tt, :] = glu(cur_ref[0])
    ext_ref[h + tt:2 * h + tt, :] = jnp.where(t < nt - 1, glu(next_ref[0]), 0.0)
    span = tt + 2 * h - sub
    for s in range(1, sub):
        sh_ref[s - 1, :, :] = ext_ref[s:s + span, :]
    rs = 64
    first = h - CONV_W // 2
    for r in range(0, tt, rs):
        acc = jnp.zeros((rs, c), F32)
        for j in range(CONV_W):
            s, a0 = (first + j) % sub, r + (first + j) // sub * sub
            rows = ext_ref[a0:a0 + rs, :] if s == 0 else sh_ref[s - 1, a0:a0 + rs, :]
            acc = acc + rows * w_ref[j:j + 1, :]
        y = acc + cb_ref[...]
        mu = jnp.mean(y, axis=-1, keepdims=True)
        dlt = y - mu
        var = jnp.mean(dlt * dlt, axis=-1, keepdims=True)
        yn = dlt * lax.rsqrt(var + LN_EPS) * lg_ref[...] + lb_ref[...]
        o_ref[0, r:r + rs, :] = _silu(yn)


def _conv_module(u, w, cb, ln_g, ln_b):
    b, t, _ = u.shape
    tt = min(t, 512)
    c = CONV_CH
    hb = tt // CONV_HALO
    last = t // CONV_HALO - 1
    wp = jnp.zeros((32, c), F32).at[:CONV_W].set(w)
    return pl.pallas_call(
        functools.partial(_conv_kernel, tt=tt),
        grid=(b, t // tt),
        in_specs=[pl.BlockSpec((1, tt, 2 * c), lambda i, j: (i, j, 0)),
                  pl.BlockSpec((1, CONV_HALO, 2 * c), lambda i, j: (i, jnp.maximum(j * hb - 1, 0), 0)),
                  pl.BlockSpec((1, CONV_HALO, 2 * c), lambda i, j: (i, jnp.minimum((j + 1) * hb, last), 0)),
                  pl.BlockSpec((32, c), lambda i, j: (0, 0)),
                  pl.BlockSpec((1, c), lambda i, j: (0, 0)),
                  pl.BlockSpec((1, c), lambda i, j: (0, 0)),
                  pl.BlockSpec((1, c), lambda i, j: (0, 0))],
        out_specs=pl.BlockSpec((1, tt, c), lambda i, j: (i, j, 0)),
        out_shape=jax.ShapeDtypeStruct((b, t, c), F32),
        scratch_shapes=[pltpu.VMEM((tt + 2 * CONV_HALO, c), F32),
                        pltpu.VMEM((7, tt + 2 * CONV_HALO - 8, c), F32)],
        compiler_params=_cparams("arbitrary", "arbitrary"),
        name="conv_module",
    )(u, u, u, wp, cb.reshape(1, c), ln_g.reshape(1, c), ln_b.reshape(1, c))


def _split3(x):
    hi = x.astype(BF16)
    r1 = x - hi.astype(F32)
    mid = r1.astype(BF16)
    lo = (r1 - mid.astype(F32)).astype(BF16)
    return hi, mid, lo


def _hgrn_scores(q, b_ref, kk_ref, d, base, lanes, rev, wsel, sub_iota, lane_iota):
    c = HG_CHUNK
    sb = HG_SUB
    nb = c // sb
    b = b_ref[d, base:base + c, lanes]
    kk = kk_ref[d, base:base + c, lanes]
    rows = []
    for i in range(nb):
        r0 = i * sb
        lo, hi = (r0 + sb, c) if rev else (0, r0)
        if hi <= lo:
            rows.append(jnp.zeros((sb, c), F32))
            continue
        edge = base + (r0 + sb if rev else r0 - 1)
        ref = b_ref[d, edge:edge + 1, lanes]
        qt = q[r0:r0 + sb, :] * jnp.exp2(b[r0:r0 + sb, :] - ref)
        kt = kk[lo:hi, :] * jnp.exp2(ref - b[lo:hi, :])
        pieces = [jnp.zeros((lo, HG_KDIM), F32), kt] if rev else [kt, jnp.zeros((c - hi, HG_KDIM), F32)]
        rows.append(_dot_nt(qt.astype(BF16), jnp.concatenate(pieces, axis=0).astype(BF16)))
    xs = []
    for i in range(nb):
        r0 = i * sb
        bq = b[r0:r0 + sb, :]
        qq = q[r0:r0 + sb, :]
        for p in range(sb // 2):
            pair = []
            for s in (2 * p, 2 * p + 1):
                keep = (sub_iota <= s) if rev else (sub_iota >= s)
                row = base + r0 + s
                e = jnp.exp2(bq - b_ref[d, row:row + 1, lanes])
                pair.append(jnp.where(keep, qq * kk_ref[d, row:row + 1, lanes] * e, 0.0))
            xs.append(jnp.concatenate(pair, axis=1))
    y = _dot(jnp.concatenate(xs, axis=0).astype(BF16), wsel)
    blocks = []
    for i in range(nb):
        r0 = i * sb
        acc = rows[i]
        for p in range(sb // 2):
            g0 = (i * (sb // 2) + p) * sb
            acc = jnp.where((lane_iota >> 1) == (r0 // 2 + p), y[g0:g0 + sb, :c], acc)
        blocks.append(acc)
    return jnp.concatenate(blocks, axis=0)


def _hgrn_kernel(*refs, layer, tc, has_s0, emit_state):
    qf_ref, zf_ref, vf_ref, qb_ref, zb_ref, vb_ref, lbraw_ref = refs[:7]
    pos = 7
    s0_ref = None
    if has_s0:
        s0_ref = refs[pos]
        pos += 1
    of_ref, ob_ref = refs[pos], refs[pos + 1]
    pos += 2
    sf_ref = None
    if emit_state:
        sf_ref = refs[pos]
        pos += 1
    st_ref, b_ref, kk_ref, qs_ref, qe_ref, kd_ref, v16_ref, dec_ref, sc_ref = refs[pos:pos + 9]

    s = pl.program_id(1)
    ns = pl.num_programs(1)
    c = HG_CHUNK
    nch = tc // c
    w = HG_WIDTH

    @pl.when(s == 0)
    def _init():
        for d in range(2):
            for h in range(HG_HEADS):
                if has_s0:
                    st_ref[d, h] = s0_ref[0, d, h].T
                else:
                    st_ref[d, h] = jnp.zeros((HG_VDIM, HG_KDIM), F32)

    def lower_bound(d):
        rows = [lbraw_ref[d, j:j + 1, :] for j in range(DEPTH + 1)]
        m = functools.reduce(jnp.maximum, rows)
        ex = [jnp.exp(r - m) for r in rows]
        tot = functools.reduce(lambda a, b_: a + b_, ex)
        return functools.reduce(lambda a, b_: a + b_, [e / tot for e in ex[:layer + 1]])

    r64 = lax.broadcasted_iota(jnp.int32, (c, c), 0)
    c64 = lax.broadcasted_iota(jnp.int32, (c, c), 1)
    tri = ((r64 >= c64).astype(BF16), (r64 <= c64).astype(BF16))
    rsel = lax.broadcasted_iota(jnp.int32, (2 * LANES, LANES), 0)
    csel = lax.broadcasted_iota(jnp.int32, (2 * LANES, LANES), 1)
    wsel = ((rsel >> 7) == (csel & 1)).astype(BF16)
    sub_iota = lax.broadcasted_iota(jnp.int32, (HG_SUB, LANES), 0)
    lane_iota = lax.broadcasted_iota(jnp.int32, (HG_SUB, c), 1)

    dirs = ((qf_ref, zf_ref, vf_ref, of_ref), (qb_ref, zb_ref, vb_ref, ob_ref))

    sums = {}
    for d, (q_ref, z_ref, v_ref, _) in enumerate(dirs):
        lb = lower_bound(d)
        for ci in range(nch):
            base = ci * c
            f = lb + (1.0 - lb) * jax.nn.sigmoid(z_ref[0, base:base + c, :])
            kk_ref[d, base:base + c, :] = 1.0 - f
            sums[d, ci] = _dot(tri[d], jnp.concatenate(_split3(jnp.log(f) * LOG2_E), axis=1))
            qs_ref[d, base:base + c, :] = _silu(q_ref[0, base:base + c, :])
            v16_ref[d, base:base + c, :] = v_ref[0, base:base + c, :].astype(BF16)
    b_low = None
    for d in range(2):
        for ci in range(nch):
            base = ci * c
            cs = sums[d, ci]
            b = cs[:, :w] + cs[:, w:2 * w] + cs[:, 2 * w:]
            b_ref[d, base:base + c, :] = b
            b_end = b[0:1, :] if d == 1 else b[c - 1:c, :]
            b_low = b_end if b_low is None else jnp.minimum(b_low, b_end)
            qe_ref[d, base:base + c, :] = (qs_ref[d, base:base + c, :] * jnp.exp2(b)).astype(BF16)
            kd_ref[d, base:base + c, :] = (kk_ref[d, base:base + c, :] * jnp.exp2(b_end - b)).astype(BF16)
            dec_ref[d, ci * HG_SUB:ci * HG_SUB + 1, :] = jnp.exp2(b_end)

    safe = jnp.min(b_low) >= -HG_SAFE_LOG2

    @pl.when(safe)
    def _fast():
        for d in range(2):
            keep = (r64 <= c64) if d == 1 else (r64 >= c64)
            for ci in range(nch):
                rows = slice(ci * c, (ci + 1) * c)
                kinv = (kk_ref[d, rows, :] * jnp.exp2(-b_ref[d, rows, :])).astype(BF16)
                for h in range(HG_HEADS):
                    sl = slice(h * HG_KDIM, (h + 1) * HG_KDIM)
                    sc = _dot_nt(qe_ref[d, rows, sl], kinv[:, sl])
                    sc_ref[d, ci, h] = jnp.where(keep, sc, 0.0).astype(BF16)

    @pl.when(jnp.logical_not(safe))
    def _exact():
        for d in range(2):
            for ci in range(nch):
                base = ci * c
                for h in range(HG_HEADS):
                    sl = slice(h * HG_KDIM, (h + 1) * HG_KDIM)
                    sc = _hgrn_scores(qs_ref[d, base:base + c, sl], b_ref, kk_ref, d, base, sl, d == 1,
                                      wsel, sub_iota, lane_iota)
                    sc_ref[d, ci, h] = sc.astype(BF16)

    chains = [(d, h) for d in range(2) for h in range(HG_HEADS)]
    states = [st_ref[d, h] for d, h in chains]
    for step in range(nch):
        for n, (d, h) in enumerate(chains):
            ci = nch - 1 - step if d == 1 else step
            sl = slice(h * HG_KDIM, (h + 1) * HG_KDIM)
            rows = slice(ci * c, (ci + 1) * c)
            v16 = v16_ref[d, rows, sl]
            st = states[n]
            dirs[d][3][0, rows, sl] = (_dot_nt(qe_ref[d, rows, sl], st.astype(BF16))
                                       + _dot(sc_ref[d, ci, h], v16))
            states[n] = st * dec_ref[d, ci * HG_SUB:ci * HG_SUB + 1, sl] + _dot_tn(v16, kd_ref[d, rows, sl])
    for n, (d, h) in enumerate(chains):
        st_ref[d, h] = states[n]

    if emit_state:
        @pl.when(s == ns - 1)
        def _fin():
            for d in range(2):
                for h in range(HG_HEADS):
                    sf_ref[0, d, h] = st_ref[d, h].T


def _hgrn_scan(u, lb_raw, s0, layer, emit_state):
    b, t, _ = u.shape
    tc = HG_TILE
    ns = t // tc
    nch = tc // HG_CHUNK
    w = HG_WIDTH
    fwd = lambda col: pl.BlockSpec((1, tc, w), lambda i, j: (i, j, col))
    bwd = lambda col: pl.BlockSpec((1, tc, w), lambda i, j: (i, ns - 1 - j, col))
    in_specs = [fwd(2), fwd(3), fwd(5), bwd(2), bwd(4), bwd(5),
                pl.BlockSpec(lb_raw.shape, lambda i, j: (0, 0, 0))]
    args = [u, u, u, u, u, u, lb_raw]
    state_spec = pl.BlockSpec((1, 2, HG_HEADS, HG_KDIM, HG_VDIM), lambda i, j: (i, 0, 0, 0, 0))
    if s0 is not None:
        in_specs.append(state_spec)
        args.append(s0)
    out_specs = [pl.BlockSpec((1, tc, w), lambda i, j: (i, j, 0)),
                 pl.BlockSpec((1, tc, w), lambda i, j: (i, ns - 1 - j, 0))]
    out_shape = [jax.ShapeDtypeStruct((b, t, w), F32), jax.ShapeDtypeStruct((b, t, w), F32)]
    if emit_state:
        out_specs.append(state_spec)
        out_shape.append(jax.ShapeDtypeStruct((b, 2, HG_HEADS, HG_KDIM, HG_VDIM), F32))
    return pl.pallas_call(
        functools.partial(_hgrn_kernel, layer=layer, tc=tc, has_s0=s0 is not None, emit_state=emit_state),
        grid=(b, ns),
        in_specs=in_specs,
        out_specs=out_specs,
        out_shape=out_shape,
        scratch_shapes=[pltpu.VMEM((2, HG_HEADS, HG_VDIM, HG_KDIM), F32),
                        pltpu.VMEM((2, tc, w), F32),
                        pltpu.VMEM((2, tc, w), F32),
                        pltpu.VMEM((2, tc, w), F32),
                        pltpu.VMEM((2, tc, w), BF16),
                        pltpu.VMEM((2, tc, w), BF16),
                        pltpu.VMEM((2, tc, w), BF16),
                        pltpu.VMEM((2, nch * HG_SUB, w), F32),
                        pltpu.VMEM((2, nch, HG_HEADS, HG_CHUNK, HG_CHUNK), BF16)],
        compiler_params=_cparams("arbitrary", "arbitrary"),
        name="hgrn_scan",
    )(*args)


def _head_norm(x, gain, seg_ones):
    parts = []
    for j in range(x.shape[1] // 256):
        xs = x[:, 256 * j:256 * (j + 1)]
        ss = _dot((xs * xs).astype(BF16), seg_ones)
        parts.append(xs * lax.rsqrt(ss * (1.0 / HEAD_DIM) + EPS))
    y = parts[0] if len(parts) == 1 else jnp.concatenate(parts, axis=1)
    return y * gain


def _rope(x, cos, sin):
    lane = lax.broadcasted_iota(jnp.int32, (1, LANES), 1)
    first = (lane & 31) < 16
    parts = []
    for j in range(x.shape[1] // LANES):
        xs = x[:, LANES * j:LANES * (j + 1)]
        partner = jnp.where(first, pltpu.roll(xs, LANES - 16, 1), pltpu.roll(xs, 16, 1))
        parts.append(xs * cos + partner * sin)
    return jnp.concatenate(parts, axis=1)


def _qkv_kernel(*refs, rope):
    x_ref, mod_ref, g_ref, w_ref, qn_ref, kn_ref = refs[:6]
    pos = 6
    if rope:
        cos_ref, sin_ref = refs[6], refs[7]
        pos = 8
    q_ref, k_ref, v_ref = refs[pos:pos + 3]
    h = _prenorm(x_ref[...], g_ref[...], mod_ref, 1).astype(BF16)
    qkv = _dot(h, w_ref[...])
    nq = N_HEADS * HEAD_DIM
    nk = N_KV_HEADS * HEAD_DIM
    r = lax.broadcasted_iota(jnp.int32, (256, 256), 0)
    c = lax.broadcasted_iota(jnp.int32, (256, 256), 1)
    seg_ones = ((r >> 6) == (c >> 6)).astype(BF16)
    q = _head_norm(qkv[:, :nq], qn_ref[...], seg_ones)
    k = _head_norm(qkv[:, nq:nq + nk], kn_ref[...], seg_ones)
    if rope:
        q = _rope(q, cos_ref[...], sin_ref[...])
        k = _rope(k, cos_ref[...], sin_ref[...])
    q_ref[...] = q * (HEAD_DIM ** -0.5 * LOG2_E)
    k_ref[...] = k
    v_ref[...] = qkv[:, nq + nk:]


def _qkv(x, mod, g, w, qn, kn, rope_tabs):
    n, d = x.shape
    tm = ROW_TILE
    per_group = (n // mod.shape[0]) // tm
    nq = N_HEADS * HEAD_DIM
    nk = N_KV_HEADS * HEAD_DIM
    in_specs = [pl.BlockSpec((tm, d), lambda i: (i, 0)),
                pl.BlockSpec((1, N_MOD, d), lambda i: (i // per_group, 0, 0)),
                pl.BlockSpec((1, d), lambda i: (0, 0)),
                _resident(w.shape),
                pl.BlockSpec((1, nq), lambda i: (0, 0)),
                pl.BlockSpec((1, nk), lambda i: (0, 0))]
    args = [x, mod, g.reshape(1, d), w, jnp.tile(qn, N_HEADS).reshape(1, nq), jnp.tile(kn, N_KV_HEADS).reshape(1, nk)]
    if rope_tabs is not None:
        cos, sin = rope_tabs
        per_seq = cos.shape[0] // tm
        in_specs += [pl.BlockSpec((tm, LANES), lambda i: (i % per_seq, 0))] * 2
        args += [cos, sin]
    return pl.pallas_call(
        functools.partial(_qkv_kernel, rope=rope_tabs is not None),
        grid=(n // tm,),
        in_specs=in_specs,
        out_specs=[pl.BlockSpec((tm, nq), lambda i: (i, 0)),
                   pl.BlockSpec((tm, nk), lambda i: (i, 0)),
                   pl.BlockSpec((tm, nk), lambda i: (i, 0))],
        out_shape=[jax.ShapeDtypeStruct((n, nq), F32),
                   jax.ShapeDtypeStruct((n, nk), F32),
                   jax.ShapeDtypeStruct((n, nk), F32)],
        compiler_params=_cparams("arbitrary"),
        name="qkv_proj",
    )(*args)


def _rope_tables(t):
    inv = ROPE_BASE ** (-jnp.arange(0, ROPE_AX, 2, dtype=F32) / ROPE_AX)
    ang_r = jnp.arange(t // GRID_W, dtype=F32)[:, None] * inv[None, :]
    ang_c = jnp.arange(GRID_W, dtype=F32)[:, None] * inv[None, :]
    by_row = lambda a: jnp.repeat(a, GRID_W, axis=0)
    by_col = lambda a: jnp.tile(a, (t // GRID_W, 1))
    cos_r, sin_r = by_row(jnp.cos(ang_r)), by_row(jnp.sin(ang_r))
    cos_c, sin_c = by_col(jnp.cos(ang_c)), by_col(jnp.sin(ang_c))
    cos = jnp.concatenate([cos_r, cos_r, cos_c, cos_c], axis=1)
    sin = jnp.concatenate([-sin_r, sin_r, -sin_c, sin_c], axis=1)
    return jnp.concatenate([cos, cos], axis=1), jnp.concatenate([sin, sin], axis=1)


def _attend(q, sections, sink_ref, tq):
    lane = lax.broadcasted_iota(jnp.int32, (1, LANES), 1)
    low = lane < HEAD_DIM
    kall = jnp.concatenate([sec[0] for sec in sections], axis=0)
    vall = jnp.concatenate([sec[1] for sec in sections], axis=0)
    ones = jnp.ones((kall.shape[0], LANES), BF16)
    slabs = [None] * (N_HEADS // 2)
    for j in range(N_KV_HEADS // 2):
        ks = kall[:, LANES * j:LANES * (j + 1)]
        vs = vall[:, LANES * j:LANES * (j + 1)]
        ks_r = pltpu.roll(ks, HEAD_DIM, 1)
        vs_r = pltpu.roll(vs, HEAD_DIM, 1)
        for e in range(2):
            kv = 2 * j + e
            own = low if e == 0 else jnp.logical_not(low)
            kboth = jnp.where(own, ks, ks_r).astype(BF16)
            vext = jnp.concatenate([jnp.where(own, vs, vs_r).astype(BF16), ones], axis=1)
            qs, sk = [], []
            for g in range(GROUP):
                h = GROUP * kv + g
                half = low if h % 2 == 0 else jnp.logical_not(low)
                qs.append(jnp.where(half, q[:, LANES * (h // 2):LANES * (h // 2 + 1)], 0.0).astype(BF16))
                sk.append(jnp.full((tq, LANES), sink_ref[h] * LOG2_E, F32))
            s = _dot_nt(jnp.concatenate(qs, axis=0), kboth)
            pieces, c0 = [], 0
            for k_sec, _, bias in sections:
                for c1 in range(c0, c0 + k_sec.shape[0], LANES):
                    piece = s[:, c1:c1 + LANES]
                    if bias is not None:
                        piece = piece + jnp.concatenate([bias[:, c1 - c0:c1 - c0 + LANES]] * GROUP, axis=0)
                    pieces.append(piece)
                c0 += k_sec.shape[0]
            sink = jnp.concatenate(sk, axis=0)
            top = functools.reduce(jnp.maximum, pieces)
            m = jnp.maximum(jnp.broadcast_to(jnp.max(top, axis=-1, keepdims=True), top.shape), sink)
            p = jnp.concatenate([jnp.exp2(piece - m).astype(BF16) for piece in pieces], axis=1)
            oe = _dot(p, vext)
            o = oe[:, :LANES] * (1.0 / (oe[:, LANES:] + jnp.exp2(sink - m)))
            slabs[2 * kv] = jnp.where(low, o[0:tq], o[tq:2 * tq])
            slabs[2 * kv + 1] = jnp.where(low, o[2 * tq:3 * tq], o[3 * tq:4 * tq])
    return jnp.concatenate(slabs, axis=1)


def _ctx_attn_kernel(sink_ref, q_ref, k_ref, v_ref, o_ref, *, tq):
    o_ref[...] = _attend(q_ref[...], [(k_ref[...], v_ref[...], None)], sink_ref, tq)


def _ctx_attention(q, k, v, sinks, seq):
    n = q.shape[0]
    nq = q.shape[1]
    nk = k.shape[1]
    return pl.pallas_call(
        functools.partial(_ctx_attn_kernel, tq=seq),
        grid=(n // seq,),
        in_specs=[pl.BlockSpec(memory_space=pltpu.SMEM),
                  pl.BlockSpec((seq, nq), lambda i: (i, 0)),
                  pl.BlockSpec((seq, nk), lambda i: (i, 0)),
                  pl.BlockSpec((seq, nk), lambda i: (i, 0))],
        out_specs=pl.BlockSpec((seq, nq), lambda i: (i, 0)),
        out_shape=jax.ShapeDtypeStruct((n, nq), F32),
        compiler_params=_cparams("arbitrary"),
        name="ctx_attention",
    )(sinks, q, k, v)


def _lat_attn_kernel(sink_ref, q_ref, ck_ref, cv_ref, kp_ref, kc_ref, kn_ref, vp_ref, vc_ref, vn_ref, o_ref, *, tq):
    n = pl.program_id(1)
    nb = pl.num_programs(1)
    qi = lax.broadcasted_iota(jnp.int32, (tq, tq), 0)
    kj = lax.broadcasted_iota(jnp.int32, (tq, tq), 1)
    bias_prev = jnp.where((kj >= qi) & (n > 0), 0.0, -jnp.inf)
    bias_next = jnp.where((kj <= qi) & (n < nb - 1), 0.0, -jnp.inf)
    sections = [(ck_ref[0], cv_ref[0], None), (kp_ref[...], vp_ref[...], bias_prev),
                (kc_ref[...], vc_ref[...], None), (kn_ref[...], vn_ref[...], bias_next)]
    o_ref[...] = _attend(q_ref[...], sections, sink_ref, tq)


def _lat_attention(q, k, v, ck, cv, sinks, seq):
    n = q.shape[0]
    nq = q.shape[1]
    nk = k.shape[1]
    tq = ATT_BLOCK
    nb = seq // tq
    nctx = ck.shape[1]
    blk = lambda shift: pl.BlockSpec((tq, nk), lambda b, j: (b * nb + jnp.clip(j + shift, 0, nb - 1), 0))
    ctx = pl.BlockSpec((1, nctx, nk), lambda b, j: (b, 0, 0))
    return pl.pallas_call(
        functools.partial(_lat_attn_kernel, tq=tq),
        grid=(n // seq, nb),
        in_specs=[pl.BlockSpec(memory_space=pltpu.SMEM),
                  pl.BlockSpec((tq, nq), lambda b, j: (b * nb + j, 0)),
                  ctx, ctx, blk(-1), blk(0), blk(1), blk(-1), blk(0), blk(1)],
        out_specs=pl.BlockSpec((tq, nq), lambda b, j: (b * nb + j, 0)),
        out_shape=jax.ShapeDtypeStruct((n, nq), F32),
        compiler_params=_cparams("arbitrary", "arbitrary"),
        name="lat_attention",
    )(sinks, q, ck, cv, k, k, k, v, v, v)


def kernel(x_prompt, x_sample, c, state_hgrn, cache_k, cache_v, c_ctx, w_mod, b_mod, norm_g, ffn_w_in, ffn_w_out, ev_w_in, ev_w_out, conv_w, conv_b, conv_ln_g, conv_ln_b, hg_lb_raw, hg_norm_g, od_w_in, od_w_out, q_norm_g, k_norm_g, sinks):
    batch, seq, d = x_prompt.shape
    dec_batch, dec_seq, _ = x_sample.shape
    depth = w_mod.shape[0]
    ffn_in = [[ffn_w_in[l, i].astype(BF16) for i in range(2)] for l in range(depth)]
    ffn_out = [[ffn_w_out[l, i].astype(BF16) for i in range(2)] for l in range(depth)]
    ev_in = ev_w_in.astype(BF16)
    ev_out = ev_w_out.astype(BF16)
    od_in = od_w_in.astype(BF16)
    od_out = od_w_out.astype(BF16)

    cond = jnp.concatenate([c_ctx[None, :], c, jnp.zeros((8 - 1 - dec_batch, d), F32)], axis=0)
    mod = _adaln(cond, w_mod, b_mod).reshape(depth, 8, N_MOD, d)
    rope_tabs = _rope_tables(dec_seq)

    streams = [
        dict(x=x_prompt.reshape(batch * seq, d), nb=batch, t=seq, lo=0, hi=1, latent=False),
        dict(x=x_sample.reshape(dec_batch * dec_seq, d), nb=dec_batch, t=dec_seq, lo=1, hi=1 + dec_batch, latent=True),
    ]
    hg_states, ks_new, vs_new = [], [], []
    for st in streams:
        x = st["x"]
        nb, t, latent = st["nb"], st["t"], st["latent"]
        for l in range(depth):
            m = mod[l, st["lo"]:st["hi"]]
            x = _ffn(x, m, norm_g[l, 0], ffn_in[l][0], ffn_out[l][0], 0)
            if l % 2 == 0:
                e = l // 2
                u = _proj(x, m, norm_g[l, 1], ev_in[e]).reshape(nb, t, EVEN_IN)
                a = _conv_module(u, conv_w[e], conv_b[e], conv_ln_g[e], conv_ln_b[e])
                s0 = state_hgrn[:, e] if latent else None
                res = _hgrn_scan(u, hg_lb_raw, s0, l, emit_state=not latent)
                if not latent:
                    hg_states.append(res[2])
                n = nb * t
                tail = ("even", (a.reshape(n, CONV_CH), res[0].reshape(n, HG_WIDTH), res[1].reshape(n, HG_WIDTH),
                                 u.reshape(n, EVEN_IN), hg_norm_g[e], ev_out[e]))
            else:
                o = l // 2
                q, k, v = _qkv(x, m, norm_g[l, 1], od_in[o], q_norm_g[o], k_norm_g[o], rope_tabs if latent else None)
                if latent:
                    nk = N_KV_HEADS * HEAD_DIM
                    ck = cache_k[:, o].reshape(nb, -1, nk)
                    cv = cache_v[:, o].reshape(nb, -1, nk)
                    att = _lat_attention(q, k, v, ck, cv, sinks[o], t)
                else:
                    att = _ctx_attention(q, k, v, sinks[o], t)
                    ks_new.append(k.reshape(nb, t, N_KV_HEADS, HEAD_DIM))
                    vs_new.append(v.reshape(nb, t, N_KV_HEADS, HEAD_DIM))
                tail = ("attn", (att, od_out[o]))
            x = _ffn(x, m, norm_g[l, 2], ffn_in[l][1], ffn_out[l][1], 2, *tail)
        st["y"] = x

    y_prompt = streams[0]["y"].reshape(batch, seq, d)
    y_sample = streams[1]["y"].reshape(dec_batch, dec_seq, d)
    new_state_hgrn = jnp.stack(hg_states, axis=1)
    new_cache_k = jnp.stack(ks_new, axis=1)
    new_cache_v = jnp.stack(vs_new, axis=1)
    return (y_prompt, y_sample, new_state_hgrn, new_cache_k, new_cache_v)
```

```python
import functools

import jax
import jax.numpy as jnp
from jax import lax
from jax.experimental import pallas as pl
from jax.experimental.pallas import tpu as pltpu

F32 = jnp.float32
BF16 = jnp.bfloat16

D_MODEL = 1024
DEPTH = 2
N_MOD = 9
D_FF = 2816
EPS = 1e-6
LN_EPS = 1e-5
LOG2_E = 1.4426950408889634
CONV_CH = 512
CONV_W = 31
CONV_HALO = 16
HG_HEADS = 4
HG_KDIM = 128
HG_VDIM = 128
HG_WIDTH = HG_HEADS * HG_KDIM
HG_CHUNK = 64
HG_SUB = 8
HG_TILE = 256
HG_SAFE_LOG2 = 96.0
EVEN_IN = 2 * CONV_CH + 5 * HG_WIDTH
N_HEADS = 16
N_KV_HEADS = 4
HEAD_DIM = 64
GROUP = N_HEADS // N_KV_HEADS
ATT_BLOCK = 128
GRID_W = 64
ROPE_AX = HEAD_DIM // 2
ROPE_BASE = 10000.0
LANES = 128
VMEM_LIMIT_BYTES = 56 * 1024 * 1024
ROW_TILE = 512


def _cparams(*sem):
    return pltpu.CompilerParams(dimension_semantics=sem, vmem_limit_bytes=VMEM_LIMIT_BYTES)


def _dot(a, b):
    return jnp.dot(a, b, preferred_element_type=F32)


def _dot_nt(a, b):
    return lax.dot_general(a, b, (((1,), (1,)), ((), ())), preferred_element_type=F32)


def _dot_tn(a, b):
    return lax.dot_general(a, b, (((0,), (0,)), ((), ())), preferred_element_type=F32)


def _silu(x):
    return x * jax.nn.sigmoid(x)


def _resident(shape, lead=()):
    nlead = len(lead)
    block = (None,) * nlead + tuple(shape[nlead:])
    return pl.BlockSpec(block, lambda *_: tuple(lead) + (0,) * (len(shape) - nlead), pipeline_mode=pl.Buffered(1))


def _prenorm(x, g, mod_ref, slot):
    shift = mod_ref[0, 3 * slot:3 * slot + 1, :]
    scale = mod_ref[0, 3 * slot + 1:3 * slot + 2, :]
    ms = jnp.mean(x * x, axis=-1, keepdims=True)
    y = x * lax.rsqrt(ms + EPS) * g
    return y * (1.0 + scale) + shift


def _mod_kernel(c_ref, w_ref, b_ref, o_ref):
    c = c_ref[...]
    o_ref[0] = _dot(_silu(c).astype(BF16), w_ref[0].astype(BF16)) + b_ref[0]


def _adaln(cond, w_mod, b_mod):
    n_layers, d, n = w_mod.shape
    r = cond.shape[0]
    tn = 1024
    return pl.pallas_call(
        _mod_kernel,
        grid=(n_layers, n // tn),
        in_specs=[pl.BlockSpec((r, d), lambda l, j: (0, 0)),
                  pl.BlockSpec((1, d, tn), lambda l, j: (l, 0, j)),
                  pl.BlockSpec((1, 1, tn), lambda l, j: (l, 0, j))],
        out_specs=pl.BlockSpec((1, r, tn), lambda l, j: (l, 0, j)),
        out_shape=jax.ShapeDtypeStruct((n_layers, r, n), F32),
        compiler_params=_cparams("arbitrary", "arbitrary"),
        name="adaln",
    )(cond, w_mod, b_mod.reshape(n_layers, 1, n))


def _ffn_chunks():
    out, c0 = [], 0
    while c0 < D_FF:
        cw = min(1024, D_FF - c0)
        out.append((c0, cw))
        c0 += cw
    return out


def _even_mix(a_ref, of_ref, ob_ref, gz_ref, ng_ref, w_ref):
    o = of_ref[...] + ob_ref[...]
    gs = _silu(gz_ref[...])
    parts = []
    for h in range(HG_HEADS):
        sl = slice(h * HG_VDIM, (h + 1) * HG_VDIM)
        oh = o[:, sl]
        ms = jnp.mean(oh * oh, axis=-1, keepdims=True)
        parts.append(oh * lax.rsqrt(ms + EPS) * ng_ref[...] * gs[:, sl])
    r = jnp.concatenate(parts, axis=1).astype(BF16)
    c = CONV_CH
    return _dot(a_ref[...].astype(BF16), w_ref[0:c, :]) + _dot(r, w_ref[c:, :])


def _ffn_kernel(*refs, slot, tail):
    x_ref, mod_ref, g_ref, win_ref, wout_ref = refs[:5]
    o_ref, act_ref = refs[-2:]
    x = x_ref[...]
    if tail == "even":
        x = x + mod_ref[0, 5:6, :] * _even_mix(*refs[5:11])
    elif tail == "attn":
        x = x + mod_ref[0, 5:6, :] * _dot(refs[5][...].astype(BF16), refs[6][...])
    h = _prenorm(x, g_ref[...], mod_ref, slot).astype(BF16)
    for c0, cw in _ffn_chunks():
        gt = _dot(h, win_ref[:, c0:c0 + cw])
        up = _dot(h, win_ref[:, D_FF + c0:D_FF + c0 + cw])
        act_ref[:, c0:c0 + cw] = (_silu(gt) * up).astype(BF16)
    out = _dot(act_ref[...], wout_ref[...])
    gate = mod_ref[0, 3 * slot + 2:3 * slot + 3, :]
    o_ref[...] = x + (0.5 * gate) * out


def _ffn(x, mod, g, w_in, w_out, which, slot, tail=None, tail_args=()):
    n, d = x.shape
    tm = ROW_TILE
    per_group = (n // mod.shape[0]) // tm
    row = lambda width: pl.BlockSpec((tm, width), lambda i: (i, 0))
    in_specs = [row(d),
                pl.BlockSpec((1, N_MOD, d), lambda i: (i // per_group, 0, 0)),
                pl.BlockSpec((1, d), lambda i: (0, 0)),
                _resident(w_in.shape, which),
                _resident(w_out.shape, which)]
    args = [x, mod, g.reshape(1, d), w_in, w_out]
    if tail == "even":
        a, o_f, o_b, u, norm_g, w_mix = tail_args
        in_specs += [row(CONV_CH), row(HG_WIDTH), row(HG_WIDTH),
                     pl.BlockSpec((tm, HG_WIDTH), lambda i: (i, 6)),
                     pl.BlockSpec((1, HG_VDIM), lambda i: (0, 0)),
                     _resident(w_mix.shape)]
        args += [a, o_f, o_b, u, norm_g.reshape(1, HG_VDIM), w_mix]
    elif tail == "attn":
        att, w_mix = tail_args
        in_specs += [row(att.shape[1]), _resident(w_mix.shape)]
        args += [att, w_mix]
    return pl.pallas_call(
        functools.partial(_ffn_kernel, slot=slot, tail=tail),
        grid=(n // tm,),
        in_specs=in_specs,
        out_specs=row(d),
        out_shape=jax.ShapeDtypeStruct((n, d), F32),
        scratch_shapes=[pltpu.VMEM((tm, D_FF), BF16)],
        compiler_params=_cparams("arbitrary"),
        name="ffn" if tail is None else "mix_ffn",
    )(*args)


def _proj_kernel(x_ref, mod_ref, g_ref, w_ref, o_ref):
    h = _prenorm(x_ref[...], g_ref[...], mod_ref, 1).astype(BF16)
    o_ref[...] = _dot(h, w_ref[...])


def _proj(x, mod, g, w):
    n, d = x.shape
    nout = w.shape[1]
    tm = ROW_TILE
    per_group = (n // mod.shape[0]) // tm
    return pl.pallas_call(
        _proj_kernel,
        grid=(n // tm,),
        in_specs=[pl.BlockSpec((tm, d), lambda i: (i, 0)),
                  pl.BlockSpec((1, N_MOD, d), lambda i: (i // per_group, 0, 0)),
                  pl.BlockSpec((1, d), lambda i: (0, 0)),
                  _resident(w.shape)],
        out_specs=pl.BlockSpec((tm, nout), lambda i: (i, 0)),
        out_shape=jax.ShapeDtypeStruct((n, nout), F32),
        compiler_params=_cparams("arbitrary"),
        name="even_proj",
    )(x, mod, g.reshape(1, d), w)


def _conv_kernel(cur_ref, prev_ref, next_ref, w_ref, cb_ref, lg_ref, lb_ref, o_ref, ext_ref, sh_ref, *, tt):
    t = pl.program_id(1)
    nt = pl.num_programs(1)
    c = CONV_CH
    h = CONV_HALO
    sub = 8

    def glu(v):
        return v[:, :c] * jax.nn.sigmoid(v[:, c:])

    ext_ref[0:h, :] = jnp.where(t > 0, glu(prev_ref[0]), 0.0)
    ext_ref[h:h + tt, :] = glu(cur_ref[0])
    ext_ref[h + tt:2 * h + tt, :] = jnp.where(t < nt - 1, glu(next_ref[0]), 0.0)
    span = tt + 2 * h - sub
    for s in range(1, sub):
        sh_ref[s - 1, :, :] = ext_ref[s:s + span, :]
    rs = 64
    first = h - CONV_W // 2
    for r in range(0, tt, rs):
        acc = jnp.zeros((rs, c), F32)
        for j in range(CONV_W):
            s, a0 = (first + j) % sub, r + (first + j) // sub * sub
            rows = ext_ref[a0:a0 + rs, :] if s == 0 else sh_ref[s - 1, a0:a0 + rs, :]
            acc = acc + rows * w_ref[j:j + 1, :]
        y = acc + cb_ref[...]
        mu = jnp.mean(y, axis=-1, keepdims=True)
        dlt = y - mu
        var = jnp.mean(dlt * dlt, axis=-1, keepdims=True)
        yn = dlt * lax.rsqrt(var + LN_EPS) * lg_ref[...] + lb_ref[...]
        o_ref[0, r:r + rs, :] = _silu(yn)


def _conv_module(u, w, cb, ln_g, ln_b):
    b, t, _ = u.shape
    tt = min(t, 512)
    c = CONV_CH
    hb = tt // CONV_HALO
    last = t // CONV_HALO - 1
    wp = jnp.zeros((32, c), F32).at[:CONV_W].set(w)
    return pl.pallas_call(
        functools.partial(_conv_kernel, tt=tt),
        grid=(b, t // tt),
        in_specs=[pl.BlockSpec((1, tt, 2 * c), lambda i, j: (i, j, 0)),
                  pl.BlockSpec((1, CONV_HALO, 2 * c), lambda i, j: (i, jnp.maximum(j * hb - 1, 0), 0)),
                  pl.BlockSpec((1, CONV_HALO, 2 * c), lambda i, j: (i, jnp.minimum((j + 1) * hb, last), 0)),
                  pl.BlockSpec((32, c), lambda i, j: (0, 0)),
                  pl.BlockSpec((1, c), lambda i, j: (0, 0)),
                  pl.BlockSpec((1, c), lambda i, j: (0, 0)),
                  pl.BlockSpec((1, c), lambda i, j: (0, 0))],
        out_specs=pl.BlockSpec((1, tt, c), lambda i, j: (i, j, 0)),
        out_shape=jax.ShapeDtypeStruct((b, t, c), F32),
        scratch_shapes=[pltpu.VMEM((tt + 2 * CONV_HALO, c), F32),
                        pltpu.VMEM((7, tt + 2 * CONV_HALO - 8, c), F32)],
        compiler_params=_cparams("arbitrary", "arbitrary"),
        name="conv_module",
    )(u, u, u, wp, cb.reshape(1, c), ln_g.reshape(1, c), ln_b.reshape(1, c))


def _split3(x):
    hi = x.astype(BF16)
    r1 = x - hi.astype(F32)
    mid = r1.astype(BF16)
    lo = (r1 - mid.astype(F32)).astype(BF16)
    return hi, mid, lo


def _hgrn_scores(q, b_ref, kk_ref, d, base, lanes, rev, wsel, sub_iota, lane_iota):
    c = HG_CHUNK
    sb = HG_SUB
    nb = c // sb
    b = b_ref[d, base:base + c, lanes]
    kk = kk_ref[d, base:base + c, lanes]
    rows = []
    for i in range(nb):
        r0 = i * sb
        lo, hi = (r0 + sb, c) if rev else (0, r0)
        if hi <= lo:
            rows.append(jnp.zeros((sb, c), F32))
            continue
        edge = base + (r0 + sb if rev else r0 - 1)
        ref = b_ref[d, edge:edge + 1, lanes]
        qt = q[r0:r0 + sb, :] * jnp.exp2(b[r0:r0 + sb, :] - ref)
        kt = kk[lo:hi, :] * jnp.exp2(ref - b[lo:hi, :])
        pieces = [jnp.zeros((lo, HG_KDIM), F32), kt] if rev else [kt, jnp.zeros((c - hi, HG_KDIM), F32)]
        rows.append(_dot_nt(qt.astype(BF16), jnp.concatenate(pieces, axis=0).astype(BF16)))
    xs = []
    for i in range(nb):
        r0 = i * sb
        bq = b[r0:r0 + sb, :]
        qq = q[r0:r0 + sb, :]
        for p in range(sb // 2):
            pair = []
            for s in (2 * p, 2 * p + 1):
                keep = (sub_iota <= s) if rev else (sub_iota >= s)
                row = base + r0 + s
                e = jnp.exp2(bq - b_ref[d, row:row + 1, lanes])
                pair.append(jnp.where(keep, qq * kk_ref[d, row:row + 1, lanes] * e, 0.0))
            xs.append(jnp.concatenate(pair, axis=1))
    y = _dot(jnp.concatenate(xs, axis=0).astype(BF16), wsel)
    blocks = []
    for i in range(nb):
        r0 = i * sb
        acc = rows[i]
        for p in range(sb // 2):
            g0 = (i * (sb // 2) + p) * sb
            acc = jnp.where((lane_iota >> 1) == (r0 // 2 + p), y[g0:g0 + sb, :c], acc)
        blocks.append(acc)
    return jnp.concatenate(blocks, axis=0)


def _hgrn_kernel(*refs, layer, tc, has_s0, emit_state):
    qf_ref, zf_ref, vf_ref, qb_ref, zb_ref, vb_ref, lbraw_ref = refs[:7]
    pos = 7
    s0_ref = None
    if has_s0:
        s0_ref = refs[pos]
        pos += 1
    of_ref, ob_ref = refs[pos], refs[pos + 1]
    pos += 2
    sf_ref = None
    if emit_state:
        sf_ref = refs[pos]
        pos += 1
    st_ref, b_ref, kk_ref, qs_ref, qe_ref, kd_ref, v16_ref, dec_ref, sc_ref = refs[pos:pos + 9]

    s = pl.program_id(1)
    ns = pl.num_programs(1)
    c = HG_CHUNK
    nch = tc // c
    w = HG_WIDTH

    @pl.when(s == 0)
    def _init():
        for d in range(2):
            for h in range(HG_HEADS):
                if has_s0:
                    st_ref[d, h] = s0_ref[0, d, h].T
                else:
                    st_ref[d, h] = jnp.zeros((HG_VDIM, HG_KDIM), F32)

    def lower_bound(d):
        rows = [lbraw_ref[d, j:j + 1, :] for j in range(DEPTH + 1)]
        m = functools.reduce(jnp.maximum, rows)
        ex = [jnp.exp(r - m) for r in rows]
        tot = functools.reduce(lambda a, b_: a + b_, ex)
        return functools.reduce(lambda a, b_: a + b_, [e / tot for e in ex[:layer + 1]])

    r64 = lax.broadcasted_iota(jnp.int32, (c, c), 0)
    c64 = lax.broadcasted_iota(jnp.int32, (c, c), 1)
    tri = ((r64 >= c64).astype(BF16), (r64 <= c64).astype(BF16))
    rsel = lax.broadcasted_iota(jnp.int32, (2 * LANES, LANES), 0)
    csel = lax.broadcasted_iota(jnp.int32, (2 * LANES, LANES), 1)
    wsel = ((rsel >> 7) == (csel & 1)).astype(BF16)
    sub_iota = lax.broadcasted_iota(jnp.int32, (HG_SUB, LANES), 0)
    lane_iota = lax.broadcasted_iota(jnp.int32, (HG_SUB, c), 1)

    dirs = ((qf_ref, zf_ref, vf_ref, of_ref), (qb_ref, zb_ref, vb_ref, ob_ref))

    sums = {}
    for d, (q_ref, z_ref, v_ref, _) in enumerate(dirs):
        lb = lower_bound(d)
        for ci in range(nch):
            base = ci * c
            f = lb + (1.0 - lb) * jax.nn.sigmoid(z_ref[0, base:base + c, :])
            kk_ref[d, base:base + c, :] = 1.0 - f
            sums[d, ci] = _dot(tri[d], jnp.concatenate(_split3(jnp.log(f) * LOG2_E), axis=1))
            qs_ref[d, base:base + c, :] = _silu(q_ref[0, base:base + c, :])
            v16_ref[d, base:base + c, :] = v_ref[0, base:base + c, :].astype(BF16)
    b_low = None
    for d in range(2):
        for ci in range(nch):
            base = ci * c
            cs = sums[d, ci]
            b = cs[:, :w] + cs[:, w:2 * w] + cs[:, 2 * w:]
            b_ref[d, base:base + c, :] = b
            b_end = b[0:1, :] if d == 1 else b[c - 1:c, :]
            b_low = b_end if b_low is None else jnp.minimum(b_low, b_end)
            qe_ref[d, base:base + c, :] = (qs_ref[d, base:base + c, :] * jnp.exp2(b)).astype(BF16)
            kd_ref[d, base:base + c, :] = (kk_ref[d, base:base + c, :] * jnp.exp2(b_end - b)).astype(BF16)
            dec_ref[d, ci * HG_SUB:ci * HG_SUB + 1, :] = jnp.exp2(b_end)

    chains = [(d, h) for d in range(2) for h in range(HG_HEADS)]

    def recur(score_of):
        states = [st_ref[d, h] for d, h in chains]
        for step in range(nch):
            for n, (d, h) in enumerate(chains):
                ci = nch - 1 - step if d == 1 else step
                sl = slice(h * HG_KDIM, (h + 1) * HG_KDIM)
                rows = slice(ci * c, (ci + 1) * c)
                v16 = v16_ref[d, rows, sl]
                st = states[n]
                dirs[d][3][0, rows, sl] = (_dot_nt(qe_ref[d, rows, sl], st.astype(BF16))
                                           + _dot(score_of(d, ci, h), v16))
                states[n] = st * dec_ref[d, ci * HG_SUB:ci * HG_SUB + 1, sl] + _dot_tn(v16, kd_ref[d, rows, sl])
        return states

    safe = jnp.min(b_low) >= -HG_SAFE_LOG2
    split = {}
    for d in range(2):
        keep = (r64 <= c64) if d == 1 else (r64 >= c64)
        for ci in range(nch):
            rows = slice(ci * c, (ci + 1) * c)
            kinv = (kk_ref[d, rows, :] * jnp.exp2(-b_ref[d, rows, :])).astype(BF16)
            for h in range(HG_HEADS):
                sl = slice(h * HG_KDIM, (h + 1) * HG_KDIM)
                sc = _dot_nt(qe_ref[d, rows, sl], kinv[:, sl])
                split[d, ci, h] = jnp.where(keep, sc, 0.0).astype(BF16)
    split_states = recur(lambda d, ci, h: split[d, ci, h])

    @pl.when(safe)
    def _keep():
        for n, (d, h) in enumerate(chains):
            st_ref[d, h] = split_states[n]

    @pl.when(jnp.logical_not(safe))
    def _exact():
        for d in range(2):
            for ci in range(nch):
                base = ci * c
                for h in range(HG_HEADS):
                    sl = slice(h * HG_KDIM, (h + 1) * HG_KDIM)
                    sc = _hgrn_scores(qs_ref[d, base:base + c, sl], b_ref, kk_ref, d, base, sl, d == 1,
                                      wsel, sub_iota, lane_iota)
                    sc_ref[d, ci, h] = sc.astype(BF16)
        exact_states = recur(lambda d, ci, h: sc_ref[d, ci, h])
        for n, (d, h) in enumerate(chains):
            st_ref[d, h] = exact_states[n]

    if emit_state:
        @pl.when(s == ns - 1)
        def _fin():
            for d in range(2):
                for h in range(HG_HEADS):
                    sf_ref[0, d, h] = st_ref[d, h].T


def _hgrn_scan(u, lb_raw, s0, layer, emit_state):
    b, t, _ = u.shape
    tc = HG_TILE
    ns = t // tc
    nch = tc // HG_CHUNK
    w = HG_WIDTH
    fwd = lambda col: pl.BlockSpec((1, tc, w), lambda i, j: (i, j, col))
    bwd = lambda col: pl.BlockSpec((1, tc, w), lambda i, j: (i, ns - 1 - j, col))
    in_specs = [fwd(2), fwd(3), fwd(5), bwd(2), bwd(4), bwd(5),
                pl.BlockSpec(lb_raw.shape, lambda i, j: (0, 0, 0))]
    args = [u, u, u, u, u, u, lb_raw]
    state_spec = pl.BlockSpec((1, 2, HG_HEADS, HG_KDIM, HG_VDIM), lambda i, j: (i, 0, 0, 0, 0))
    if s0 is not None:
        in_specs.append(state_spec)
        args.append(s0)
    out_specs = [pl.BlockSpec((1, tc, w), lambda i, j: (i, j, 0)),
                 pl.BlockSpec((1, tc, w), lambda i, j: (i, ns - 1 - j, 0))]
    out_shape = [jax.ShapeDtypeStruct((b, t, w), F32), jax.ShapeDtypeStruct((b, t, w), F32)]
    if emit_state:
        out_specs.append(state_spec)
        out_shape.append(jax.ShapeDtypeStruct((b, 2, HG_HEADS, HG_KDIM, HG_VDIM), F32))
    return pl.pallas_call(
        functools.partial(_hgrn_kernel, layer=layer, tc=tc, has_s0=s0 is not None, emit_state=emit_state),
        grid=(b, ns),
        in_specs=in_specs,
        out_specs=out_specs,
        out_shape=out_shape,
        scratch_shapes=[pltpu.VMEM((2, HG_HEADS, HG_VDIM, HG_KDIM), F32),
                        pltpu.VMEM((2, tc, w), F32),
                        pltpu.VMEM((2, tc, w), F32),
                        pltpu.VMEM((2, tc, w), F32),
                        pltpu.VMEM((2, tc, w), BF16),
                        pltpu.VMEM((2, tc, w), BF16),
                        pltpu.VMEM((2, tc, w), BF16),
                        pltpu.VMEM((2, nch * HG_SUB, w), F32),
                        pltpu.VMEM((2, nch, HG_HEADS, HG_CHUNK, HG_CHUNK), BF16)],
        compiler_params=_cparams("arbitrary", "arbitrary"),
        name="hgrn_scan",
    )(*args)


def _head_norm(x, gain, seg_ones):
    ss = _dot((x * x).astype(BF16), seg_ones)
    return x * lax.rsqrt(ss * (1.0 / HEAD_DIM) + EPS) * gain


def _rope(x, cos, sin):
    lane = lax.broadcasted_iota(jnp.int32, (1, LANES), 1)
    first = (lane & 31) < 16
    parts = []
    for j in range(x.shape[1] // LANES):
        xs = x[:, LANES * j:LANES * (j + 1)]
        partner = jnp.where(first, pltpu.roll(xs, LANES - 16, 1), pltpu.roll(xs, 16, 1))
        parts.append(xs * cos + partner * sin)
    return jnp.concatenate(parts, axis=1)


def _qkv_kernel(*refs, rope):
    x_ref, mod_ref, g_ref, w_ref, qn_ref, kn_ref = refs[:6]
    pos = 6
    if rope:
        cos_ref, sin_ref = refs[6], refs[7]
        pos = 8
    q_ref, k_ref, v_ref = refs[pos:pos + 3]
    h = _prenorm(x_ref[...], g_ref[...], mod_ref, 1).astype(BF16)
    nq = N_HEADS * HEAD_DIM
    nk = N_KV_HEADS * HEAD_DIM
    sw = 256
    cw = 2 * sw
    r = lax.broadcasted_iota(jnp.int32, (sw, sw), 0)
    c = lax.broadcasted_iota(jnp.int32, (sw, sw), 1)
    seg_ones = ((r >> 6) == (c >> 6)).astype(BF16)
    for c0 in range(0, nq + 2 * nk, cw):
        y2 = _dot(h, w_ref[:, c0:c0 + cw])
        for c1 in range(c0, c0 + cw, sw):
            y = y2[:, c1 - c0:c1 - c0 + sw]
            if c1 >= nq + nk:
                v_ref[:, c1 - nq - nk:c1 - nq - nk + sw] = y
                continue
            is_q = c1 < nq
            gain = qn_ref[:, c1:c1 + sw] if is_q else kn_ref[:, c1 - nq:c1 - nq + sw]
            y = _head_norm(y, gain, seg_ones)
            if rope:
                y = _rope(y, cos_ref[...], sin_ref[...])
            if is_q:
                q_ref[:, c1:c1 + sw] = y * (HEAD_DIM ** -0.5 * LOG2_E)
            else:
                k_ref[:, c1 - nq:c1 - nq + sw] = y


def _qkv(x, mod, g, w, qn, kn, rope_tabs):
    n, d = x.shape
    tm = ROW_TILE
    per_group = (n // mod.shape[0]) // tm
    nq = N_HEADS * HEAD_DIM
    nk = N_KV_HEADS * HEAD_DIM
    in_specs = [pl.BlockSpec((tm, d), lambda i: (i, 0)),
                pl.BlockSpec((1, N_MOD, d), lambda i: (i // per_group, 0, 0)),
                pl.BlockSpec((1, d), lambda i: (0, 0)),
                _resident(w.shape),
                pl.BlockSpec((1, nq), lambda i: (0, 0)),
                pl.BlockSpec((1, nk), lambda i: (0, 0))]
    args = [x, mod, g.reshape(1, d), w, jnp.tile(qn, N_HEADS).reshape(1, nq), jnp.tile(kn, N_KV_HEADS).reshape(1, nk)]
    if rope_tabs is not None:
        cos, sin = rope_tabs
        per_seq = cos.shape[0] // tm
        in_specs += [pl.BlockSpec((tm, LANES), lambda i: (i % per_seq, 0))] * 2
        args += [cos, sin]
    return pl.pallas_call(
        functools.partial(_qkv_kernel, rope=rope_tabs is not None),
        grid=(n // tm,),
        in_specs=in_specs,
        out_specs=[pl.BlockSpec((tm, nq), lambda i: (i, 0)),
                   pl.BlockSpec((tm, nk), lambda i: (i, 0)),
                   pl.BlockSpec((tm, nk), lambda i: (i, 0))],
        out_shape=[jax.ShapeDtypeStruct((n, nq), F32),
                   jax.ShapeDtypeStruct((n, nk), F32),
                   jax.ShapeDtypeStruct((n, nk), F32)],
        compiler_params=_cparams("arbitrary"),
        name="qkv_proj",
    )(*args)


def _rope_tables(t):
    inv = ROPE_BASE ** (-jnp.arange(0, ROPE_AX, 2, dtype=F32) / ROPE_AX)
    ang_r = jnp.arange(t // GRID_W, dtype=F32)[:, None] * inv[None, :]
    ang_c = jnp.arange(GRID_W, dtype=F32)[:, None] * inv[None, :]
    by_row = lambda a: jnp.repeat(a, GRID_W, axis=0)
    by_col = lambda a: jnp.tile(a, (t // GRID_W, 1))
    cos_r, sin_r = by_row(jnp.cos(ang_r)), by_row(jnp.sin(ang_r))
    cos_c, sin_c = by_col(jnp.cos(ang_c)), by_col(jnp.sin(ang_c))
    cos = jnp.concatenate([cos_r, cos_r, cos_c, cos_c], axis=1)
    sin = jnp.concatenate([-sin_r, sin_r, -sin_c, sin_c], axis=1)
    return jnp.concatenate([cos, cos], axis=1), jnp.concatenate([sin, sin], axis=1)


def _attend(q, sections, sink_ref, tq):
    lane = lax.broadcasted_iota(jnp.int32, (1, LANES), 1)
    low = lane < HEAD_DIM
    kall = jnp.concatenate([sec[0] for sec in sections], axis=0)
    vall = jnp.concatenate([sec[1] for sec in sections], axis=0)
    ones = jnp.ones((kall.shape[0], LANES), BF16)
    nslab = N_KV_HEADS // 2
    k16 = [kall[:, LANES * j:LANES * (j + 1)].astype(BF16) for j in range(nslab)]
    vext = [jnp.concatenate([vall[:, LANES * j:LANES * (j + 1)].astype(BF16), ones], axis=1) for j in range(nslab)]
    halves = (low, jnp.logical_not(low))

    scores, sinks = [], []
    for kv in range(N_KV_HEADS):
        qs, sk = [], []
        for g in range(GROUP):
            h = GROUP * kv + g
            slab = q[:, LANES * (h // 2):LANES * (h // 2 + 1)]
            if h % 2 != kv % 2:
                slab = pltpu.roll(slab, HEAD_DIM, 1)
            qs.append(jnp.where(halves[kv % 2], slab, 0.0).astype(BF16))
            sk.append(jnp.full((tq, LANES), sink_ref[h] * LOG2_E, F32))
        scores.append(_dot_nt(jnp.concatenate(qs, axis=0), k16[kv // 2]))
        sinks.append(jnp.concatenate(sk, axis=0))

    probs, tops = [], []
    for kv in range(N_KV_HEADS):
        pieces, c0 = [], 0
        for k_sec, _, bias in sections:
            for c1 in range(c0, c0 + k_sec.shape[0], LANES):
                piece = scores[kv][:, c1:c1 + LANES]
                if bias is not None:
                    piece = piece + jnp.concatenate([bias[:, c1 - c0:c1 - c0 + LANES]] * GROUP, axis=0)
                pieces.append(piece)
            c0 += k_sec.shape[0]
        top = functools.reduce(jnp.maximum, pieces)
        m = jnp.maximum(jnp.broadcast_to(jnp.max(top, axis=-1, keepdims=True), top.shape), sinks[kv])
        probs.append(jnp.concatenate([jnp.exp2(piece - m).astype(BF16) for piece in pieces], axis=1))
        tops.append(m)

    outs = [_dot(probs[kv], vext[kv // 2]) for kv in range(N_KV_HEADS)]

    slabs = []
    for kv in range(N_KV_HEADS):
        oe = outs[kv]
        o = oe[:, :LANES] * (1.0 / (oe[:, LANES:] + jnp.exp2(sinks[kv] - tops[kv])))
        heads = []
        for g in range(GROUP):
            og = o[g * tq:(g + 1) * tq]
            heads.append(og if g % 2 == kv % 2 else pltpu.roll(og, HEAD_DIM, 1))
        slabs.append(jnp.where(low, heads[0], heads[1]))
        slabs.append(jnp.where(low, heads[2], heads[3]))
    return jnp.concatenate(slabs, axis=1)


def _ctx_attn_kernel(sink_ref, q_ref, k_ref, v_ref, o_ref, *, tq):
    o_ref[...] = _attend(q_ref[...], [(k_ref[...], v_ref[...], None)], sink_ref, tq)


def _ctx_attention(q, k, v, sinks, seq):
    n = q.shape[0]
    nq = q.shape[1]
    nk = k.shape[1]
    return pl.pallas_call(
        functools.partial(_ctx_attn_kernel, tq=seq),
        grid=(n // seq,),
        in_specs=[pl.BlockSpec(memory_space=pltpu.SMEM),
                  pl.BlockSpec((seq, nq), lambda i: (i, 0)),
                  pl.BlockSpec((seq, nk), lambda i: (i, 0)),
                  pl.BlockSpec((seq, nk), lambda i: (i, 0))],
        out_specs=pl.BlockSpec((seq, nq), lambda i: (i, 0)),
        out_shape=jax.ShapeDtypeStruct((n, nq), F32),
        compiler_params=_cparams("arbitrary"),
        name="ctx_attention",
    )(sinks, q, k, v)


def _lat_attn_kernel(sink_ref, q_ref, ck_ref, cv_ref, kp_ref, kc_ref, kn_ref, vp_ref, vc_ref, vn_ref, o_ref, *, tq):
    n = pl.program_id(1)
    nb = pl.num_programs(1)
    qi = lax.broadcasted_iota(jnp.int32, (tq, tq), 0)
    kj = lax.broadcasted_iota(jnp.int32, (tq, tq), 1)
    bias_prev = jnp.where((kj >= qi) & (n > 0), 0.0, -jnp.inf)
    bias_next = jnp.where((kj <= qi) & (n < nb - 1), 0.0, -jnp.inf)
    sections = [(ck_ref[0], cv_ref[0], None), (kp_ref[...], vp_ref[...], bias_prev),
                (kc_ref[...], vc_ref[...], None), (kn_ref[...], vn_ref[...], bias_next)]
    o_ref[...] = _attend(q_ref[...], sections, sink_ref, tq)


def _lat_attention(q, k, v, ck, cv, sinks, seq):
    n = q.shape[0]
    nq = q.shape[1]
    nk = k.shape[1]
    tq = ATT_BLOCK
    nb = seq // tq
    nctx = ck.shape[1]
    blk = lambda shift: pl.BlockSpec((tq, nk), lambda b, j: (b * nb + jnp.clip(j + shift, 0, nb - 1), 0))
    ctx = pl.BlockSpec((1, nctx, nk), lambda b, j: (b, 0, 0))
    return pl.pallas_call(
        functools.partial(_lat_attn_kernel, tq=tq),
        grid=(n // seq, nb),
        in_specs=[pl.BlockSpec(memory_space=pltpu.SMEM),
                  pl.BlockSpec((tq, nq), lambda b, j: (b * nb + j, 0)),
                  ctx, ctx, blk(-1), blk(0), blk(1), blk(-1), blk(0), blk(1)],
        out_specs=pl.BlockSpec((tq, nq), lambda b, j: (b * nb + j, 0)),
        out_shape=jax.ShapeDtypeStruct((n, nq), F32),
        compiler_params=_cparams("arbitrary", "arbitrary"),
        name="lat_attention",
    )(sinks, q, ck, cv, k, k, k, v, v, v)


def kernel(x_prompt, x_sample, c, state_hgrn, cache_k, cache_v, c_ctx, w_mod, b_mod, norm_g, ffn_w_in, ffn_w_out, ev_w_in, ev_w_out, conv_w, conv_b, conv_ln_g, conv_ln_b, hg_lb_raw, hg_norm_g, od_w_in, od_w_out, q_norm_g, k_norm_g, sinks):
    batch, seq, d = x_prompt.shape
    dec_batch, dec_seq, _ = x_sample.shape
    depth = w_mod.shape[0]
    ffn_in = ffn_w_in.astype(BF16)
    ffn_out = ffn_w_out.astype(BF16)
    ev_in = ev_w_in.astype(BF16)
    ev_out = ev_w_out.astype(BF16)
    od_in = od_w_in.astype(BF16)
    od_out = od_w_out.astype(BF16)

    cond = jnp.concatenate([c_ctx[None, :], c, jnp.zeros((8 - 1 - dec_batch, d), F32)], axis=0)
    mod = _adaln(cond, w_mod, b_mod).reshape(depth, 8, N_MOD, d)
    rope_tabs = _rope_tables(dec_seq)

    streams = [
        dict(x=x_prompt.reshape(batch * seq, d), nb=batch, t=seq, lo=0, hi=1, latent=False),
        dict(x=x_sample.reshape(dec_batch * dec_seq, d), nb=dec_batch, t=dec_seq, lo=1, hi=1 + dec_batch, latent=True),
    ]
    hg_states, ks_new, vs_new = [], [], []
    for st in streams:
        x = st["x"]
        nb, t, latent = st["nb"], st["t"], st["latent"]
        for l in range(depth):
            m = mod[l, st["lo"]:st["hi"]]
            x = _ffn(x, m, norm_g[l, 0], ffn_in, ffn_out, (l, 0), 0)
            if l % 2 == 0:
                e = l // 2
                u = _proj(x, m, norm_g[l, 1], ev_in[e]).reshape(nb, t, EVEN_IN)
                a = _conv_module(u, conv_w[e], conv_b[e], conv_ln_g[e], conv_ln_b[e])
                s0 = state_hgrn[:, e] if latent else None
                res = _hgrn_scan(u, hg_lb_raw, s0, l, emit_state=not latent)
                if not latent:
                    hg_states.append(res[2])
                n = nb * t
                tail = ("even", (a.reshape(n, CONV_CH), res[0].reshape(n, HG_WIDTH), res[1].reshape(n, HG_WIDTH),
                                 u.reshape(n, EVEN_IN), hg_norm_g[e], ev_out[e]))
            else:
                o = l // 2
                q, k, v = _qkv(x, m, norm_g[l, 1], od_in[o], q_norm_g[o], k_norm_g[o], rope_tabs if latent else None)
                if latent:
                    nk = N_KV_HEADS * HEAD_DIM
                    ck = cache_k[:, o].reshape(nb, -1, nk)
                    cv = cache_v[:, o].reshape(nb, -1, nk)
                    att = _lat_attention(q, k, v, ck, cv, sinks[o], t)
                else:
                    att = _ctx_attention(q, k, v, sinks[o], t)
                    ks_new.append(k.reshape(nb, t, N_KV_HEADS, HEAD_DIM))
                    vs_new.append(v.reshape(nb, t, N_KV_HEADS, HEAD_DIM))
                tail = ("attn", (att, od_out[o]))
            x = _ffn(x, m, norm_g[l, 2], ffn_in, ffn_out, (l, 1), 2, *tail)
        st["y"] = x

    y_prompt = streams[0]["y"].reshape(batch, seq, d)
    y_sample = streams[1]["y"].reshape(dec_batch, dec_seq, d)
    new_state_hgrn = jnp.stack(hg_states, axis=1)
    new_cache_k = jnp.stack(ks_new, axis=1)
    new_cache_v = jnp.stack(vs_new, axis=1)
    return (y_prompt, y_sample, new_state_hgrn, new_cache_k, new_cache_v)
```

```python
import functools

import jax
import jax.numpy as jnp
from jax import lax
from jax.experimental import pallas as pl
from jax.experimental.pallas import tpu as pltpu

F32 = jnp.float32
BF16 = jnp.bfloat16

D_MODEL = 1024
DEPTH = 2
N_MOD = 9
D_FF = 2816
EPS = 1e-6
LN_EPS = 1e-5
LOG2_E = 1.4426950408889634
CONV_CH = 512
CONV_W = 31
CONV_HALO = 16
HG_HEADS = 4
HG_KDIM = 128
HG_VDIM = 128
HG_WIDTH = HG_HEADS * HG_KDIM
HG_CHUNK = 64
HG_SUB = 8
HG_TILE = 256
HG_SAFE_LOG2 = 96.0
EVEN_IN = 2 * CONV_CH + 5 * HG_WIDTH
N_HEADS = 16
N_KV_HEADS = 4
HEAD_DIM = 64
GROUP = N_HEADS // N_KV_HEADS
ATT_BLOCK = 128
GRID_W = 64
ROPE_AX = HEAD_DIM // 2
ROPE_BASE = 10000.0
LANES = 128
VMEM_LIMIT_BYTES = 56 * 1024 * 1024
ROW_TILE = 512


def _cparams(*sem):
    return pltpu.CompilerParams(dimension_semantics=sem, vmem_limit_bytes=VMEM_LIMIT_BYTES)


def _dot(a, b):
    return jnp.dot(a, b, preferred_element_type=F32)


def _dot_nt(a, b):
    return lax.dot_general(a, b, (((1,), (1,)), ((), ())), preferred_element_type=F32)


def _dot_tn(a, b):
    return lax.dot_general(a, b, (((0,), (0,)), ((), ())), preferred_element_type=F32)


def _silu(x):
    return x * jax.nn.sigmoid(x)


def _resident(shape, lead=()):
    nlead = len(lead)
    block = (None,) * nlead + tuple(shape[nlead:])
    return pl.BlockSpec(block, lambda *_: tuple(lead) + (0,) * (len(shape) - nlead), pipeline_mode=pl.Buffered(1))


def _prenorm(x, g, mod_ref, slot):
    shift = mod_ref[0, 3 * slot:3 * slot + 1, :]
    scale = mod_ref[0, 3 * slot + 1:3 * slot + 2, :]
    ms = jnp.mean(x * x, axis=-1, keepdims=True)
    y = x * lax.rsqrt(ms + EPS) * g
    return y * (1.0 + scale) + shift


def _mod_kernel(c_ref, w_ref, b_ref, o_ref):
    c = c_ref[...]
    o_ref[0] = _dot(_silu(c).astype(BF16), w_ref[0].astype(BF16)) + b_ref[0]


def _adaln(cond, w_mod, b_mod):
    n_layers, d, n = w_mod.shape
    r = cond.shape[0]
    tn = 1024
    return pl.pallas_call(
        _mod_kernel,
        grid=(n_layers, n // tn),
        in_specs=[pl.BlockSpec((r, d), lambda l, j: (0, 0)),
                  pl.BlockSpec((1, d, tn), lambda l, j: (l, 0, j)),
                  pl.BlockSpec((1, 1, tn), lambda l, j: (l, 0, j))],
        out_specs=pl.BlockSpec((1, r, tn), lambda l, j: (l, 0, j)),
        out_shape=jax.ShapeDtypeStruct((n_layers, r, n), F32),
        compiler_params=_cparams("arbitrary", "arbitrary"),
        name="adaln",
    )(cond, w_mod, b_mod.reshape(n_layers, 1, n))


def _ffn_chunks():
    out, c0 = [], 0
    while c0 < D_FF:
        cw = min(1024, D_FF - c0)
        out.append((c0, cw))
        c0 += cw
    return out


def _even_mix(a_ref, of_ref, ob_ref, gz_ref, ng_ref, w_ref):
    o = of_ref[...] + ob_ref[...]
    gs = _silu(gz_ref[...])
    parts = []
    for h in range(HG_HEADS):
        sl = slice(h * HG_VDIM, (h + 1) * HG_VDIM)
        oh = o[:, sl]
        ms = jnp.mean(oh * oh, axis=-1, keepdims=True)
        parts.append(oh * lax.rsqrt(ms + EPS) * ng_ref[...] * gs[:, sl])
    r = jnp.concatenate(parts, axis=1).astype(BF16)
    c = CONV_CH
    return _dot(a_ref[...].astype(BF16), w_ref[0:c, :]) + _dot(r, w_ref[c:, :])


def _ffn_kernel(*refs, slot, tail):
    x_ref, mod_ref, g_ref, win_ref, wout_ref = refs[:5]
    o_ref, act_ref = refs[-2:]
    x = x_ref[...]
    if tail == "even":
        x = x + mod_ref[0, 5:6, :] * _even_mix(*refs[5:11])
    elif tail == "attn":
        x = x + mod_ref[0, 5:6, :] * _dot(refs[5][...].astype(BF16), refs[6][...])
    h = _prenorm(x, g_ref[...], mod_ref, slot).astype(BF16)
    for c0, cw in _ffn_chunks():
        gt = _dot(h, win_ref[:, c0:c0 + cw])
        up = _dot(h, win_ref[:, D_FF + c0:D_FF + c0 + cw])
        act_ref[:, c0:c0 + cw] = (_silu(gt) * up).astype(BF16)
    out = _dot(act_ref[...], wout_ref[...])
    gate = mod_ref[0, 3 * slot + 2:3 * slot + 3, :]
    o_ref[...] = x + (0.5 * gate) * out


def _ffn(x, mod, g, w_in, w_out, which, slot, tail=None, tail_args=()):
    n, d = x.shape
    tm = ROW_TILE
    per_group = (n // mod.shape[0]) // tm
    row = lambda width: pl.BlockSpec((tm, width), lambda i: (i, 0))
    in_specs = [row(d),
                pl.BlockSpec((1, N_MOD, d), lambda i: (i // per_group, 0, 0)),
                pl.BlockSpec((1, d), lambda i: (0, 0)),
                _resident(w_in.shape, which),
                _resident(w_out.shape, which)]
    args = [x, mod, g.reshape(1, d), w_in, w_out]
    if tail == "even":
        a, o_f, o_b, gz, norm_g, w_mix = tail_args
        in_specs += [row(CONV_CH), row(HG_WIDTH), row(HG_WIDTH), row(HG_WIDTH),
                     pl.BlockSpec((1, HG_VDIM), lambda i: (0, 0)),
                     _resident(w_mix.shape)]
        args += [a, o_f, o_b, gz, norm_g.reshape(1, HG_VDIM), w_mix]
    elif tail == "attn":
        att, w_mix = tail_args
        in_specs += [row(att.shape[1]), _resident(w_mix.shape)]
        args += [att, w_mix]
    return pl.pallas_call(
        functools.partial(_ffn_kernel, slot=slot, tail=tail),
        grid=(n // tm,),
        in_specs=in_specs,
        out_specs=row(d),
        out_shape=jax.ShapeDtypeStruct((n, d), F32),
        scratch_shapes=[pltpu.VMEM((tm, D_FF), BF16)],
        compiler_params=_cparams("arbitrary"),
        name="ffn" if tail is None else "mix_ffn",
    )(*args)


def _proj_kernel(x_ref, mod_ref, g_ref, w_ref, *o_refs):
    h = _prenorm(x_ref[...], g_ref[...], mod_ref, 1).astype(BF16)
    c0 = 0
    for o_ref in o_refs:
        cw = o_ref.shape[1]
        o_ref[...] = _dot(h, w_ref[:, c0:c0 + cw])
        c0 += cw


def _proj(x, mod, g, w):
    n, d = x.shape
    widths = (2 * CONV_CH,) + (HG_WIDTH,) * 5
    tm = ROW_TILE
    per_group = (n // mod.shape[0]) // tm
    return pl.pallas_call(
        _proj_kernel,
        grid=(n // tm,),
        in_specs=[pl.BlockSpec((tm, d), lambda i: (i, 0)),
                  pl.BlockSpec((1, N_MOD, d), lambda i: (i // per_group, 0, 0)),
                  pl.BlockSpec((1, d), lambda i: (0, 0)),
                  _resident(w.shape)],
        out_specs=[pl.BlockSpec((tm, cw), lambda i: (i, 0)) for cw in widths],
        out_shape=[jax.ShapeDtypeStruct((n, cw), F32) for cw in widths],
        compiler_params=_cparams("arbitrary"),
        name="even_proj",
    )(x, mod, g.reshape(1, d), w)


def _conv_kernel(cur_ref, prev_ref, next_ref, w_ref, cb_ref, lg_ref, lb_ref, o_ref, ext_ref, sh_ref, *, tt):
    t = pl.program_id(1)
    nt = pl.num_programs(1)
    c = CONV_CH
    h = CONV_HALO
    sub = 8

    def glu(v):
        return v[:, :c] * jax.nn.sigmoid(v[:, c:])

    ext_ref[0:h, :] = jnp.where(t > 0, glu(prev_ref[0]), 0.0)
    ext_ref[h:h + tt, :] = glu(cur_ref[0])
    ext_ref[h + tt:2 * h + tt, :] = jnp.where(t < nt - 1, glu(next_ref[0]), 0.0)
    span = tt + 2 * h - sub
    for s in range(1, sub):
        sh_ref[s - 1, :, :] = ext_ref[s:s + span, :]
    rs = 64
    first = h - CONV_W // 2
    for r in range(0, tt, rs):
        acc = jnp.zeros((rs, c), F32)
        for j in range(CONV_W):
            s, a0 = (first + j) % sub, r + (first + j) // sub * sub
            rows = ext_ref[a0:a0 + rs, :] if s == 0 else sh_ref[s - 1, a0:a0 + rs, :]
            acc = acc + rows * w_ref[j:j + 1, :]
        y = acc + cb_ref[...]
        mu = jnp.mean(y, axis=-1, keepdims=True)
        dlt = y - mu
        var = jnp.mean(dlt * dlt, axis=-1, keepdims=True)
        yn = dlt * lax.rsqrt(var + LN_EPS) * lg_ref[...] + lb_ref[...]
        o_ref[0, r:r + rs, :] = _silu(yn)


def _conv_module(u, w, cb, ln_g, ln_b):
    b, t, _ = u.shape
    tt = min(t, 512)
    c = CONV_CH
    hb = tt // CONV_HALO
    last = t // CONV_HALO - 1
    wp = jnp.zeros((32, c), F32).at[:CONV_W].set(w)
    return pl.pallas_call(
        functools.partial(_conv_kernel, tt=tt),
        grid=(b, t // tt),
        in_specs=[pl.BlockSpec((1, tt, 2 * c), lambda i, j: (i, j, 0)),
                  pl.BlockSpec((1, CONV_HALO, 2 * c), lambda i, j: (i, jnp.maximum(j * hb - 1, 0), 0)),
                  pl.BlockSpec((1, CONV_HALO, 2 * c), lambda i, j: (i, jnp.minimum((j + 1) * hb, last), 0)),
                  pl.BlockSpec((32, c), lambda i, j: (0, 0)),
                  pl.BlockSpec((1, c), lambda i, j: (0, 0)),
                  pl.BlockSpec((1, c), lambda i, j: (0, 0)),
                  pl.BlockSpec((1, c), lambda i, j: (0, 0))],
        out_specs=pl.BlockSpec((1, tt, c), lambda i, j: (i, j, 0)),
        out_shape=jax.ShapeDtypeStruct((b, t, c), F32),
        scratch_shapes=[pltpu.VMEM((tt + 2 * CONV_HALO, c), F32),
                        pltpu.VMEM((7, tt + 2 * CONV_HALO - 8, c), F32)],
        compiler_params=_cparams("arbitrary", "arbitrary"),
        name="conv_module",
    )(u, u, u, wp, cb.reshape(1, c), ln_g.reshape(1, c), ln_b.reshape(1, c))


def _split3(x):
    hi = x.astype(BF16)
    r1 = x - hi.astype(F32)
    mid = r1.astype(BF16)
    lo = (r1 - mid.astype(F32)).astype(BF16)
    return hi, mid, lo


def _hgrn_scores(q, b_ref, kk_ref, d, base, lanes, rev, wsel, sub_iota, lane_iota):
    c = HG_CHUNK
    sb = HG_SUB
    nb = c // sb
    b = b_ref[d, base:base + c, lanes]
    kk = kk_ref[d, base:base + c, lanes]
    rows = []
    for i in range(nb):
        r0 = i * sb
        lo, hi = (r0 + sb, c) if rev else (0, r0)
        if hi <= lo:
            rows.append(jnp.zeros((sb, c), F32))
            continue
        edge = base + (r0 + sb if rev else r0 - 1)
        ref = b_ref[d, edge:edge + 1, lanes]
        qt = q[r0:r0 + sb, :] * jnp.exp2(b[r0:r0 + sb, :] - ref)
        kt = kk[lo:hi, :] * jnp.exp2(ref - b[lo:hi, :])
        pieces = [jnp.zeros((lo, HG_KDIM), F32), kt] if rev else [kt, jnp.zeros((c - hi, HG_KDIM), F32)]
        rows.append(_dot_nt(qt.astype(BF16), jnp.concatenate(pieces, axis=0).astype(BF16)))
    xs = []
    for i in range(nb):
        r0 = i * sb
        bq = b[r0:r0 + sb, :]
        qq = q[r0:r0 + sb, :]
        for p in range(sb // 2):
            pair = []
            for s in (2 * p, 2 * p + 1):
                keep = (sub_iota <= s) if rev else (sub_iota >= s)
                row = base + r0 + s
                e = jnp.exp2(bq - b_ref[d, row:row + 1, lanes])
                pair.append(jnp.where(keep, qq * kk_ref[d, row:row + 1, lanes] * e, 0.0))
            xs.append(jnp.concatenate(pair, axis=1))
    y = _dot(jnp.concatenate(xs, axis=0).astype(BF16), wsel)
    blocks = []
    for i in range(nb):
        r0 = i * sb
        acc = rows[i]
        for p in range(sb // 2):
            g0 = (i * (sb // 2) + p) * sb
            acc = jnp.where((lane_iota >> 1) == (r0 // 2 + p), y[g0:g0 + sb, :c], acc)
        blocks.append(acc)
    return jnp.concatenate(blocks, axis=0)


def _hgrn_kernel(*refs, layer, tc, has_s0, emit_state):
    qf_ref, zf_ref, vf_ref, qb_ref, zb_ref, vb_ref, lbraw_ref = refs[:7]
    pos = 7
    s0_ref = None
    if has_s0:
        s0_ref = refs[pos]
        pos += 1
    of_ref, ob_ref = refs[pos], refs[pos + 1]
    pos += 2
    sf_ref = None
    if emit_state:
        sf_ref = refs[pos]
        pos += 1
    st_ref, b_ref, kk_ref, qs_ref, qe_ref, kd_ref, v16_ref, dec_ref, sc_ref = refs[pos:pos + 9]

    s = pl.program_id(1)
    ns = pl.num_programs(1)
    c = HG_CHUNK
    nch = tc // c
    w = HG_WIDTH

    @pl.when(s == 0)
    def _init():
        for d in range(2):
            for h in range(HG_HEADS):
                if has_s0:
                    st_ref[d, h] = s0_ref[0, d, h].T
                else:
                    st_ref[d, h] = jnp.zeros((HG_VDIM, HG_KDIM), F32)

    def lower_bound(d):
        rows = [lbraw_ref[d, j:j + 1, :] for j in range(DEPTH + 1)]
        m = functools.reduce(jnp.maximum, rows)
        ex = [jnp.exp(r - m) for r in rows]
        tot = functools.reduce(lambda a, b_: a + b_, ex)
        return functools.reduce(lambda a, b_: a + b_, [e / tot for e in ex[:layer + 1]])

    r64 = lax.broadcasted_iota(jnp.int32, (c, c), 0)
    c64 = lax.broadcasted_iota(jnp.int32, (c, c), 1)
    tri = ((r64 >= c64).astype(BF16), (r64 <= c64).astype(BF16))
    rsel = lax.broadcasted_iota(jnp.int32, (2 * LANES, LANES), 0)
    csel = lax.broadcasted_iota(jnp.int32, (2 * LANES, LANES), 1)
    wsel = ((rsel >> 7) == (csel & 1)).astype(BF16)
    sub_iota = lax.broadcasted_iota(jnp.int32, (HG_SUB, LANES), 0)
    lane_iota = lax.broadcasted_iota(jnp.int32, (HG_SUB, c), 1)

    dirs = ((qf_ref, zf_ref, vf_ref, of_ref), (qb_ref, zb_ref, vb_ref, ob_ref))

    sums = {}
    for d, (q_ref, z_ref, v_ref, _) in enumerate(dirs):
        lb = lower_bound(d)
        for ci in range(nch):
            base = ci * c
            f = lb + (1.0 - lb) * jax.nn.sigmoid(z_ref[0, base:base + c, :])
            kk_ref[d, base:base + c, :] = 1.0 - f
            sums[d, ci] = _dot(tri[d], jnp.concatenate(_split3(jnp.log(f) * LOG2_E), axis=1))
            qs_ref[d, base:base + c, :] = _silu(q_ref[0, base:base + c, :])
            v16_ref[d, base:base + c, :] = v_ref[0, base:base + c, :].astype(BF16)
    b_low = None
    for d in range(2):
        for ci in range(nch):
            base = ci * c
            cs = sums[d, ci]
            b = cs[:, :w] + cs[:, w:2 * w] + cs[:, 2 * w:]
            b_ref[d, base:base + c, :] = b
            b_end = b[0:1, :] if d == 1 else b[c - 1:c, :]
            b_low = b_end if b_low is None else jnp.minimum(b_low, b_end)
            qe_ref[d, base:base + c, :] = (qs_ref[d, base:base + c, :] * jnp.exp2(b)).astype(BF16)
            kd_ref[d, base:base + c, :] = (kk_ref[d, base:base + c, :] * jnp.exp2(b_end - b)).astype(BF16)
            dec_ref[d, ci * HG_SUB:ci * HG_SUB + 1, :] = jnp.exp2(b_end)

    chains = [(d, h) for d in range(2) for h in range(HG_HEADS)]

    def recur(score_of):
        states = [st_ref[d, h] for d, h in chains]
        for step in range(nch):
            for n, (d, h) in enumerate(chains):
                ci = nch - 1 - step if d == 1 else step
                sl = slice(h * HG_KDIM, (h + 1) * HG_KDIM)
                rows = slice(ci * c, (ci + 1) * c)
                v16 = v16_ref[d, rows, sl]
                st = states[n]
                dirs[d][3][0, rows, sl] = (_dot_nt(qe_ref[d, rows, sl], st.astype(BF16))
                                           + _dot(score_of(d, ci, h), v16))
                states[n] = st * dec_ref[d, ci * HG_SUB:ci * HG_SUB + 1, sl] + _dot_tn(v16, kd_ref[d, rows, sl])
        return states

    safe = jnp.min(b_low) >= -HG_SAFE_LOG2
    split = {}
    for d in range(2):
        keep = (r64 <= c64) if d == 1 else (r64 >= c64)
        for ci in range(nch):
            rows = slice(ci * c, (ci + 1) * c)
            kinv = (kk_ref[d, rows, :] * jnp.exp2(-b_ref[d, rows, :])).astype(BF16)
            for h in range(HG_HEADS):
                sl = slice(h * HG_KDIM, (h + 1) * HG_KDIM)
                sc = _dot_nt(qe_ref[d, rows, sl], kinv[:, sl])
                split[d, ci, h] = jnp.where(keep, sc, 0.0).astype(BF16)
    split_states = recur(lambda d, ci, h: split[d, ci, h])

    @pl.when(safe)
    def _keep():
        for n, (d, h) in enumerate(chains):
            st_ref[d, h] = split_states[n]

    @pl.when(jnp.logical_not(safe))
    def _exact():
        for d in range(2):
            for ci in range(nch):
                base = ci * c
                for h in range(HG_HEADS):
                    sl = slice(h * HG_KDIM, (h + 1) * HG_KDIM)
                    sc = _hgrn_scores(qs_ref[d, base:base + c, sl], b_ref, kk_ref, d, base, sl, d == 1,
                                      wsel, sub_iota, lane_iota)
                    sc_ref[d, ci, h] = sc.astype(BF16)
        exact_states = recur(lambda d, ci, h: sc_ref[d, ci, h])
        for n, (d, h) in enumerate(chains):
            st_ref[d, h] = exact_states[n]

    if emit_state:
        @pl.when(s == ns - 1)
        def _fin():
            for d in range(2):
                for h in range(HG_HEADS):
                    sf_ref[0, d, h] = st_ref[d, h].T


def _hgrn_scan(q, zf, zb, iv, lb_raw, s0, layer, emit_state):
    b, t, w = q.shape
    tc = HG_TILE
    ns = t // tc
    nch = tc // HG_CHUNK
    fwd = pl.BlockSpec((1, tc, w), lambda i, j: (i, j, 0))
    bwd = pl.BlockSpec((1, tc, w), lambda i, j: (i, ns - 1 - j, 0))
    in_specs = [fwd, fwd, fwd, bwd, bwd, bwd, pl.BlockSpec(lb_raw.shape, lambda i, j: (0, 0, 0))]
    args = [q, zf, iv, q, zb, iv, lb_raw]
    state_spec = pl.BlockSpec((1, 2, HG_HEADS, HG_KDIM, HG_VDIM), lambda i, j: (i, 0, 0, 0, 0))
    if s0 is not None:
        in_specs.append(state_spec)
        args.append(s0)
    out_specs = [fwd, bwd]
    out_shape = [jax.ShapeDtypeStruct((b, t, w), F32), jax.ShapeDtypeStruct((b, t, w), F32)]
    if emit_state:
        out_specs.append(state_spec)
        out_shape.append(jax.ShapeDtypeStruct((b, 2, HG_HEADS, HG_KDIM, HG_VDIM), F32))
    return pl.pallas_call(
        functools.partial(_hgrn_kernel, layer=layer, tc=tc, has_s0=s0 is not None, emit_state=emit_state),
        grid=(b, ns),
        in_specs=in_specs,
        out_specs=out_specs,
        out_shape=out_shape,
        scratch_shapes=[pltpu.VMEM((2, HG_HEADS, HG_VDIM, HG_KDIM), F32),
                        pltpu.VMEM((2, tc, w), F32),
                        pltpu.VMEM((2, tc, w), F32),
                        pltpu.VMEM((2, tc, w), F32),
                        pltpu.VMEM((2, tc, w), BF16),
                        pltpu.VMEM((2, tc, w), BF16),
                        pltpu.VMEM((2, tc, w), BF16),
                        pltpu.VMEM((2, nch * HG_SUB, w), F32),
                        pltpu.VMEM((2, nch, HG_HEADS, HG_CHUNK, HG_CHUNK), BF16)],
        compiler_params=_cparams("arbitrary", "arbitrary"),
        name="hgrn_scan",
    )(*args)


def _head_norm(x, gain, seg_ones):
    ss = _dot((x * x).astype(BF16), seg_ones)
    return x * lax.rsqrt(ss * (1.0 / HEAD_DIM) + EPS) * gain


def _rope(x, cos, sin):
    lane = lax.broadcasted_iota(jnp.int32, (1, LANES), 1)
    first = (lane & 31) < 16
    parts = []
    for j in range(x.shape[1] // LANES):
        xs = x[:, LANES * j:LANES * (j + 1)]
        partner = jnp.where(first, pltpu.roll(xs, LANES - 16, 1), pltpu.roll(xs, 16, 1))
        parts.append(xs * cos + partner * sin)
    return jnp.concatenate(parts, axis=1)


def _qkv_kernel(*refs, rope):
    x_ref, mod_ref, g_ref, w_ref, qn_ref, kn_ref = refs[:6]
    pos = 6
    if rope:
        cos_ref, sin_ref = refs[6], refs[7]
        pos = 8
    q_ref, k_ref, v_ref = refs[pos:pos + 3]
    h = _prenorm(x_ref[...], g_ref[...], mod_ref, 1).astype(BF16)
    nq = N_HEADS * HEAD_DIM
    nk = N_KV_HEADS * HEAD_DIM
    sw = 256
    cw = 2 * sw
    r = lax.broadcasted_iota(jnp.int32, (sw, sw), 0)
    c = lax.broadcasted_iota(jnp.int32, (sw, sw), 1)
    seg_ones = ((r >> 6) == (c >> 6)).astype(BF16)
    for c0 in range(0, nq + 2 * nk, cw):
        y2 = _dot(h, w_ref[:, c0:c0 + cw])
        for c1 in range(c0, c0 + cw, sw):
            y = y2[:, c1 - c0:c1 - c0 + sw]
            if c1 >= nq + nk:
                v_ref[:, c1 - nq - nk:c1 - nq - nk + sw] = y
                continue
            is_q = c1 < nq
            gain = qn_ref[:, c1:c1 + sw] if is_q else kn_ref[:, c1 - nq:c1 - nq + sw]
            y = _head_norm(y, gain, seg_ones)
            if rope:
                y = _rope(y, cos_ref[...], sin_ref[...])
            if is_q:
                q_ref[:, c1:c1 + sw] = y * (HEAD_DIM ** -0.5 * LOG2_E)
            else:
                k_ref[:, c1 - nq:c1 - nq + sw] = y


def _qkv(x, mod, g, w, qn, kn, rope_tabs):
    n, d = x.shape
    tm = ROW_TILE
    per_group = (n // mod.shape[0]) // tm
    nq = N_HEADS * HEAD_DIM
    nk = N_KV_HEADS * HEAD_DIM
    in_specs = [pl.BlockSpec((tm, d), lambda i: (i, 0)),
                pl.BlockSpec((1, N_MOD, d), lambda i: (i // per_group, 0, 0)),
                pl.BlockSpec((1, d), lambda i: (0, 0)),
                _resident(w.shape),
                pl.BlockSpec((1, nq), lambda i: (0, 0)),
                pl.BlockSpec((1, nk), lambda i: (0, 0))]
    args = [x, mod, g.reshape(1, d), w, jnp.tile(qn, N_HEADS).reshape(1, nq), jnp.tile(kn, N_KV_HEADS).reshape(1, nk)]
    if rope_tabs is not None:
        cos, sin = rope_tabs
        per_seq = cos.shape[0] // tm
        in_specs += [pl.BlockSpec((tm, LANES), lambda i: (i % per_seq, 0))] * 2
        args += [cos, sin]
    return pl.pallas_call(
        functools.partial(_qkv_kernel, rope=rope_tabs is not None),
        grid=(n // tm,),
        in_specs=in_specs,
        out_specs=[pl.BlockSpec((tm, nq), lambda i: (i, 0)),
                   pl.BlockSpec((tm, nk), lambda i: (i, 0)),
                   pl.BlockSpec((tm, nk), lambda i: (i, 0))],
        out_shape=[jax.ShapeDtypeStruct((n, nq), F32),
                   jax.ShapeDtypeStruct((n, nk), F32),
                   jax.ShapeDtypeStruct((n, nk), F32)],
        compiler_params=_cparams("arbitrary"),
        name="qkv_proj",
    )(*args)


def _rope_tables(t):
    lane = jnp.arange(LANES)
    inv = ROPE_BASE ** (-jnp.arange(0, ROPE_AX, 2, dtype=F32) / ROPE_AX)
    freq = inv[lane % (ROPE_AX // 2)][None, :]
    rows = t // GRID_W
    ang_r = jnp.arange(rows, dtype=F32)[:, None] * freq
    ang_c = jnp.arange(GRID_W, dtype=F32)[:, None] * freq
    by_row = lambda a: jnp.repeat(a, GRID_W, axis=0)
    by_col = lambda a: jnp.tile(a, (rows, 1))
    row_lane = (lane % HEAD_DIM < ROPE_AX)[None, :]
    cos = jnp.where(row_lane, by_row(jnp.cos(ang_r)), by_col(jnp.cos(ang_c)))
    sin = jnp.where(row_lane, by_row(jnp.sin(ang_r)), by_col(jnp.sin(ang_c)))
    return cos, jnp.where((lane % ROPE_AX < ROPE_AX // 2)[None, :], -sin, sin)


def _attend(q, sections, sink_ref, tq):
    lane = lax.broadcasted_iota(jnp.int32, (1, LANES), 1)
    low = lane < HEAD_DIM
    kall = jnp.concatenate([sec[0] for sec in sections], axis=0)
    vall = jnp.concatenate([sec[1] for sec in sections], axis=0)
    ones = jnp.ones((kall.shape[0], LANES), BF16)
    nslab = N_KV_HEADS // 2
    k16 = [kall[:, LANES * j:LANES * (j + 1)].astype(BF16) for j in range(nslab)]
    vext = [jnp.concatenate([vall[:, LANES * j:LANES * (j + 1)].astype(BF16), ones], axis=1) for j in range(nslab)]
    halves = (low, jnp.logical_not(low))

    scores, sinks = [], []
    for kv in range(N_KV_HEADS):
        qs, sk = [], []
        for g in range(GROUP):
            h = GROUP * kv + g
            slab = q[:, LANES * (h // 2):LANES * (h // 2 + 1)]
            if h % 2 != kv % 2:
                slab = pltpu.roll(slab, HEAD_DIM, 1)
            qs.append(jnp.where(halves[kv % 2], slab, 0.0).astype(BF16))
            sk.append(jnp.full((tq, LANES), sink_ref[h] * LOG2_E, F32))
        scores.append(_dot_nt(jnp.concatenate(qs, axis=0), k16[kv // 2]))
        sinks.append(jnp.concatenate(sk, axis=0))

    probs, tops = [], []
    for kv in range(N_KV_HEADS):
        pieces, c0 = [], 0
        for k_sec, _, bias in sections:
            for c1 in range(c0, c0 + k_sec.shape[0], LANES):
                piece = scores[kv][:, c1:c1 + LANES]
                if bias is not None:
                    piece = piece + jnp.concatenate([bias[:, c1 - c0:c1 - c0 + LANES]] * GROUP, axis=0)
                pieces.append(piece)
            c0 += k_sec.shape[0]
        top = functools.reduce(jnp.maximum, pieces)
        m = jnp.maximum(jnp.broadcast_to(jnp.max(top, axis=-1, keepdims=True), top.shape), sinks[kv])
        probs.append(jnp.concatenate([jnp.exp2(piece - m).astype(BF16) for piece in pieces], axis=1))
        tops.append(m)

    outs = [_dot(probs[kv], vext[kv // 2]) for kv in range(N_KV_HEADS)]

    slabs = []
    for kv in range(N_KV_HEADS):
        oe = outs[kv]
        o = oe[:, :LANES] * (1.0 / (oe[:, LANES:] + jnp.exp2(sinks[kv] - tops[kv])))
        heads = []
        for g in range(GROUP):
            og = o[g * tq:(g + 1) * tq]
            heads.append(og if g % 2 == kv % 2 else pltpu.roll(og, HEAD_DIM, 1))
        slabs.append(jnp.where(low, heads[0], heads[1]))
        slabs.append(jnp.where(low, heads[2], heads[3]))
    return jnp.concatenate(slabs, axis=1)


def _ctx_attn_kernel(sink_ref, q_ref, k_ref, v_ref, o_ref, *, tq):
    o_ref[...] = _attend(q_ref[...], [(k_ref[...], v_ref[...], None)], sink_ref, tq)


def _ctx_attention(q, k, v, sinks, seq):
    n = q.shape[0]
    nq = q.shape[1]
    nk = k.shape[1]
    return pl.pallas_call(
        functools.partial(_ctx_attn_kernel, tq=seq),
        grid=(n // seq,),
        in_specs=[pl.BlockSpec(memory_space=pltpu.SMEM),
                  pl.BlockSpec((seq, nq), lambda i: (i, 0)),
                  pl.BlockSpec((seq, nk), lambda i: (i, 0)),
                  pl.BlockSpec((seq, nk), lambda i: (i, 0))],
        out_specs=pl.BlockSpec((seq, nq), lambda i: (i, 0)),
        out_shape=jax.ShapeDtypeStruct((n, nq), F32),
        compiler_params=_cparams("arbitrary"),
        name="ctx_attention",
    )(sinks, q, k, v)


def _lat_attn_kernel(sink_ref, q_ref, ck_ref, cv_ref, kp_ref, kc_ref, kn_ref, vp_ref, vc_ref, vn_ref, o_ref, *, tq):
    n = pl.program_id(1)
    nb = pl.num_programs(1)
    qi = lax.broadcasted_iota(jnp.int32, (tq, tq), 0)
    kj = lax.broadcasted_iota(jnp.int32, (tq, tq), 1)
    bias_prev = jnp.where((kj >= qi) & (n > 0), 0.0, -jnp.inf)
    bias_next = jnp.where((kj <= qi) & (n < nb - 1), 0.0, -jnp.inf)
    sections = [(ck_ref[0], cv_ref[0], None), (kp_ref[...], vp_ref[...], bias_prev),
                (kc_ref[...], vc_ref[...], None), (kn_ref[...], vn_ref[...], bias_next)]
    o_ref[...] = _attend(q_ref[...], sections, sink_ref, tq)


def _lat_attention(q, k, v, ck, cv, sinks, seq):
    n = q.shape[0]
    nq = q.shape[1]
    nk = k.shape[1]
    tq = ATT_BLOCK
    nb = seq // tq
    nctx = ck.shape[1]
    blk = lambda shift: pl.BlockSpec((tq, nk), lambda b, j: (b * nb + jnp.clip(j + shift, 0, nb - 1), 0))
    ctx = pl.BlockSpec((1, nctx, nk), lambda b, j: (b, 0, 0))
    return pl.pallas_call(
        functools.partial(_lat_attn_kernel, tq=tq),
        grid=(n // seq, nb),
        in_specs=[pl.BlockSpec(memory_space=pltpu.SMEM),
                  pl.BlockSpec((tq, nq), lambda b, j: (b * nb + j, 0)),
                  ctx, ctx, blk(-1), blk(0), blk(1), blk(-1), blk(0), blk(1)],
        out_specs=pl.BlockSpec((tq, nq), lambda b, j: (b * nb + j, 0)),
        out_shape=jax.ShapeDtypeStruct((n, nq), F32),
        compiler_params=_cparams("arbitrary", "arbitrary"),
        name="lat_attention",
    )(sinks, q, ck, cv, k, k, k, v, v, v)


def kernel(x_prompt, x_sample, c, state_hgrn, cache_k, cache_v, c_ctx, w_mod, b_mod, norm_g, ffn_w_in, ffn_w_out, ev_w_in, ev_w_out, conv_w, conv_b, conv_ln_g, conv_ln_b, hg_lb_raw, hg_norm_g, od_w_in, od_w_out, q_norm_g, k_norm_g, sinks):
    batch, seq, d = x_prompt.shape
    dec_batch, dec_seq, _ = x_sample.shape
    depth = w_mod.shape[0]
    ffn_in = ffn_w_in.astype(BF16)
    ffn_out = ffn_w_out.astype(BF16)
    ev_in = ev_w_in.astype(BF16)
    ev_out = ev_w_out.astype(BF16)
    od_in = od_w_in.astype(BF16)
    od_out = od_w_out.astype(BF16)

    cond = jnp.concatenate([c_ctx[None, :], c, jnp.zeros((8 - 1 - dec_batch, d), F32)], axis=0)
    mod = _adaln(cond, w_mod, b_mod).reshape(depth, 8, N_MOD, d)
    rope_tabs = _rope_tables(dec_seq)

    streams = [
        dict(x=x_prompt.reshape(batch * seq, d), nb=batch, t=seq, lo=0, hi=1, latent=False),
        dict(x=x_sample.reshape(dec_batch * dec_seq, d), nb=dec_batch, t=dec_seq, lo=1, hi=1 + dec_batch, latent=True),
    ]
    hg_states, ks_new, vs_new = [], [], []
    for st in streams:
        x = st["x"]
        nb, t, latent = st["nb"], st["t"], st["latent"]
        for l in range(depth):
            m = mod[l, st["lo"]:st["hi"]]
            x = _ffn(x, m, norm_g[l, 0], ffn_in, ffn_out, (l, 0), 0)
            if l % 2 == 0:
                e = l // 2
                glu, hq, hzf, hzb, hi, hgz = _proj(x, m, norm_g[l, 1], ev_in[e])
                a = _conv_module(glu.reshape(nb, t, 2 * CONV_CH), conv_w[e], conv_b[e], conv_ln_g[e], conv_ln_b[e])
                s0 = state_hgrn[:, e] if latent else None
                seqs = [z.reshape(nb, t, HG_WIDTH) for z in (hq, hzf, hzb, hi)]
                res = _hgrn_scan(*seqs, hg_lb_raw, s0, l, emit_state=not latent)
                if not latent:
                    hg_states.append(res[2])
                n = nb * t
                tail = ("even", (a.reshape(n, CONV_CH), res[0].reshape(n, HG_WIDTH), res[1].reshape(n, HG_WIDTH),
                                 hgz, hg_norm_g[e], ev_out[e]))
            else:
                o = l // 2
                q, k, v = _qkv(x, m, norm_g[l, 1], od_in[o], q_norm_g[o], k_norm_g[o], rope_tabs if latent else None)
                if latent:
                    nk = N_KV_HEADS * HEAD_DIM
                    ck = cache_k[:, o].reshape(nb, -1, nk)
                    cv = cache_v[:, o].reshape(nb, -1, nk)
                    att = _lat_attention(q, k, v, ck, cv, sinks[o], t)
                else:
                    att = _ctx_attention(q, k, v, sinks[o], t)
                    ks_new.append(k.reshape(nb, t, N_KV_HEADS, HEAD_DIM))
                    vs_new.append(v.reshape(nb, t, N_KV_HEADS, HEAD_DIM))
                tail = ("attn", (att, od_out[o]))
            x = _ffn(x, m, norm_g[l, 2], ffn_in, ffn_out, (l, 1), 2, *tail)
        st["y"] = x

    y_prompt = streams[0]["y"].reshape(batch, seq, d)
    y_sample = streams[1]["y"].reshape(dec_batch, dec_seq, d)
    new_state_hgrn = jnp.stack(hg_states, axis=1)
    new_cache_k = jnp.stack(ks_new, axis=1)
    new_cache_v = jnp.stack(vs_new, axis=1)
    return (y_prompt, y_sample, new_state_hgrn, new_cache_k, new_cache_v)
```

```python
import functools

import jax
import jax.numpy as jnp
from jax import lax
from jax.experimental import pallas as pl
from jax.experimental.pallas import tpu as pltpu

F32 = jnp.float32
BF16 = jnp.bfloat16

D_MODEL = 1024
DEPTH = 2
N_MOD = 9
D_FF = 2816
EPS = 1e-6
LN_EPS = 1e-5
LOG2_E = 1.4426950408889634
CONV_CH = 512
CONV_W = 31
CONV_HALO = 16
HG_HEADS = 4
HG_KDIM = 128
HG_VDIM = 128
HG_WIDTH = HG_HEADS * HG_KDIM
HG_CHUNK = 64
HG_SUB = 8
HG_TILE = 512
HG_SAFE_LOG2 = 96.0
EVEN_IN = 2 * CONV_CH + 5 * HG_WIDTH
N_HEADS = 16
N_KV_HEADS = 4
HEAD_DIM = 64
GROUP = N_HEADS // N_KV_HEADS
ATT_BLOCK = 128
GRID_W = 64
ROPE_AX = HEAD_DIM // 2
ROPE_BASE = 10000.0
LANES = 128
VMEM_LIMIT_BYTES = 56 * 1024 * 1024
ROW_TILE = 512
FFN_ROW_TILE = 1024


def _cparams(*sem):
    return pltpu.CompilerParams(dimension_semantics=sem, vmem_limit_bytes=VMEM_LIMIT_BYTES)


def _dot(a, b):
    return jnp.dot(a, b, preferred_element_type=F32)


def _dot_nt(a, b):
    return lax.dot_general(a, b, (((1,), (1,)), ((), ())), preferred_element_type=F32)


def _dot_tn(a, b):
    return lax.dot_general(a, b, (((0,), (0,)), ((), ())), preferred_element_type=F32)


def _silu(x):
    return x * jax.nn.sigmoid(x)


def _resident(shape, lead=()):
    nlead = len(lead)
    block = (None,) * nlead + tuple(shape[nlead:])
    return pl.BlockSpec(block, lambda *_: tuple(lead) + (0,) * (len(shape) - nlead), pipeline_mode=pl.Buffered(1))


def _prenorm(x, g, mod_ref, slot):
    shift = mod_ref[0, 3 * slot:3 * slot + 1, :]
    scale = mod_ref[0, 3 * slot + 1:3 * slot + 2, :]
    ms = jnp.mean(x * x, axis=-1, keepdims=True)
    y = x * lax.rsqrt(ms + EPS) * g
    return y * (1.0 + scale) + shift


def _mod_kernel(c_ref, w_ref, b_ref, o_ref):
    c = c_ref[...]
    o_ref[0] = _dot(_silu(c).astype(BF16), w_ref[0].astype(BF16)) + b_ref[0]


def _adaln(cond, w_mod, b_mod):
    n_layers, d, n = w_mod.shape
    r = cond.shape[0]
    tn = 1024
    return pl.pallas_call(
        _mod_kernel,
        grid=(n_layers, n // tn),
        in_specs=[pl.BlockSpec((r, d), lambda l, j: (0, 0)),
                  pl.BlockSpec((1, d, tn), lambda l, j: (l, 0, j)),
                  pl.BlockSpec((1, 1, tn), lambda l, j: (l, 0, j))],
        out_specs=pl.BlockSpec((1, r, tn), lambda l, j: (l, 0, j)),
        out_shape=jax.ShapeDtypeStruct((n_layers, r, n), F32),
        compiler_params=_cparams("arbitrary", "arbitrary"),
        name="adaln",
    )(cond, w_mod, b_mod.reshape(n_layers, 1, n))


def _ffn_chunks():
    out, c0 = [], 0
    while c0 < D_FF:
        cw = min(1024, D_FF - c0)
        out.append((c0, cw))
        c0 += cw
    return out


def _even_mix(a_ref, of_ref, ob_ref, gz_ref, ng_ref, w_ref):
    o = of_ref[...] + ob_ref[...]
    gs = _silu(gz_ref[...])
    parts = []
    for h in range(HG_HEADS):
        sl = slice(h * HG_VDIM, (h + 1) * HG_VDIM)
        oh = o[:, sl]
        ms = jnp.mean(oh * oh, axis=-1, keepdims=True)
        parts.append(oh * lax.rsqrt(ms + EPS) * ng_ref[...] * gs[:, sl])
    r = jnp.concatenate(parts, axis=1).astype(BF16)
    c = CONV_CH
    return _dot(a_ref[...].astype(BF16), w_ref[0:c, :]) + _dot(r, w_ref[c:, :])


def _ffn_kernel(*refs, slot, tail):
    x_ref, mod_ref, g_ref, win_ref, wout_ref = refs[:5]
    o_ref, act_ref = refs[-2:]
    x = x_ref[...]
    if tail == "even":
        x = x + mod_ref[0, 5:6, :] * _even_mix(*refs[5:11])
    elif tail == "attn":
        x = x + mod_ref[0, 5:6, :] * _dot(refs[5][...].astype(BF16), refs[6][...])
    h = _prenorm(x, g_ref[...], mod_ref, slot).astype(BF16)
    for c0, cw in _ffn_chunks():
        gt = _dot(h, win_ref[:, c0:c0 + cw])
        up = _dot(h, win_ref[:, D_FF + c0:D_FF + c0 + cw])
        act_ref[:, c0:c0 + cw] = (_silu(gt) * up).astype(BF16)
    out = _dot(act_ref[...], wout_ref[...])
    gate = mod_ref[0, 3 * slot + 2:3 * slot + 3, :]
    o_ref[...] = x + (0.5 * gate) * out


def _ffn(x, mod, g, w_in, w_out, which, slot, tail=None, tail_args=()):
    n, d = x.shape
    tm = FFN_ROW_TILE if tail is None else ROW_TILE
    per_group = (n // mod.shape[0]) // tm
    row = lambda width: pl.BlockSpec((tm, width), lambda i: (i, 0))
    in_specs = [row(d),
                pl.BlockSpec((1, N_MOD, d), lambda i: (i // per_group, 0, 0)),
                pl.BlockSpec((1, d), lambda i: (0, 0)),
                _resident(w_in.shape, which),
                _resident(w_out.shape, which)]
    args = [x, mod, g.reshape(1, d), w_in, w_out]
    if tail == "even":
        a, o_f, o_b, gz, norm_g, w_mix = tail_args
        in_specs += [row(CONV_CH), row(HG_WIDTH), row(HG_WIDTH), row(HG_WIDTH),
                     pl.BlockSpec((1, HG_VDIM), lambda i: (0, 0)),
                     _resident(w_mix.shape)]
        args += [a, o_f, o_b, gz, norm_g.reshape(1, HG_VDIM), w_mix]
    elif tail == "attn":
        att, w_mix = tail_args
        in_specs += [row(att.shape[1]), _resident(w_mix.shape)]
        args += [att, w_mix]
    return pl.pallas_call(
        functools.partial(_ffn_kernel, slot=slot, tail=tail),
        grid=(n // tm,),
        in_specs=in_specs,
        out_specs=row(d),
        out_shape=jax.ShapeDtypeStruct((n, d), F32),
        scratch_shapes=[pltpu.VMEM((tm, D_FF), BF16)],
        compiler_params=_cparams("arbitrary"),
        name="ffn" if tail is None else "mix_ffn",
    )(*args)


def _proj_kernel(x_ref, mod_ref, g_ref, w_ref, *o_refs):
    h = _prenorm(x_ref[...], g_ref[...], mod_ref, 1).astype(BF16)
    c0 = 0
    for o_ref in o_refs:
        cw = o_ref.shape[1]
        o_ref[...] = _dot(h, w_ref[:, c0:c0 + cw])
        c0 += cw


def _proj(x, mod, g, w):
    n, d = x.shape
    widths = (2 * CONV_CH,) + (HG_WIDTH,) * 5
    tm = FFN_ROW_TILE
    per_group = (n // mod.shape[0]) // tm
    return pl.pallas_call(
        _proj_kernel,
        grid=(n // tm,),
        in_specs=[pl.BlockSpec((tm, d), lambda i: (i, 0)),
                  pl.BlockSpec((1, N_MOD, d), lambda i: (i // per_group, 0, 0)),
                  pl.BlockSpec((1, d), lambda i: (0, 0)),
                  _resident(w.shape)],
        out_specs=[pl.BlockSpec((tm, cw), lambda i: (i, 0)) for cw in widths],
        out_shape=[jax.ShapeDtypeStruct((n, cw), F32) for cw in widths],
        compiler_params=_cparams("arbitrary"),
        name="even_proj",
    )(x, mod, g.reshape(1, d), w)


def _conv_kernel(cur_ref, prev_ref, next_ref, w_ref, cb_ref, lg_ref, lb_ref, o_ref, ext_ref, sh_ref, *, tt):
    t = pl.program_id(1)
    nt = pl.num_programs(1)
    c = CONV_CH
    h = CONV_HALO
    sub = 8

    def glu(v):
        return v[:, :c] * jax.nn.sigmoid(v[:, c:])

    ext_ref[0:h, :] = jnp.where(t > 0, glu(prev_ref[0]), 0.0)
    ext_ref[h:h + tt, :] = glu(cur_ref[0])
    ext_ref[h + tt:2 * h + tt, :] = jnp.where(t < nt - 1, glu(next_ref[0]), 0.0)
    span = tt + 2 * h - sub
    for s in range(1, sub):
        sh_ref[s - 1, :, :] = ext_ref[s:s + span, :]
    rs = 64
    first = h - CONV_W // 2
    for r in range(0, tt, rs):
        acc = jnp.zeros((rs, c), F32)
        for j in range(CONV_W):
            s, a0 = (first + j) % sub, r + (first + j) // sub * sub
            rows = ext_ref[a0:a0 + rs, :] if s == 0 else sh_ref[s - 1, a0:a0 + rs, :]
            acc = acc + rows * w_ref[j:j + 1, :]
        y = acc + cb_ref[...]
        mu = jnp.mean(y, axis=-1, keepdims=True)
        dlt = y - mu
        var = jnp.mean(dlt * dlt, axis=-1, keepdims=True)
        yn = dlt * lax.rsqrt(var + LN_EPS) * lg_ref[...] + lb_ref[...]
        o_ref[0, r:r + rs, :] = _silu(yn)


def _conv_module(u, w, cb, ln_g, ln_b):
    b, t, _ = u.shape
    tt = min(t, 512)
    c = CONV_CH
    hb = tt // CONV_HALO
    last = t // CONV_HALO - 1
    wp = jnp.zeros((32, c), F32).at[:CONV_W].set(w)
    return pl.pallas_call(
        functools.partial(_conv_kernel, tt=tt),
        grid=(b, t // tt),
        in_specs=[pl.BlockSpec((1, tt, 2 * c), lambda i, j: (i, j, 0)),
                  pl.BlockSpec((1, CONV_HALO, 2 * c), lambda i, j: (i, jnp.maximum(j * hb - 1, 0), 0)),
                  pl.BlockSpec((1, CONV_HALO, 2 * c), lambda i, j: (i, jnp.minimum((j + 1) * hb, last), 0)),
                  pl.BlockSpec((32, c), lambda i, j: (0, 0)),
                  pl.BlockSpec((1, c), lambda i, j: (0, 0)),
                  pl.BlockSpec((1, c), lambda i, j: (0, 0)),
                  pl.BlockSpec((1, c), lambda i, j: (0, 0))],
        out_specs=pl.BlockSpec((1, tt, c), lambda i, j: (i, j, 0)),
        out_shape=jax.ShapeDtypeStruct((b, t, c), F32),
        scratch_shapes=[pltpu.VMEM((tt + 2 * CONV_HALO, c), F32),
                        pltpu.VMEM((7, tt + 2 * CONV_HALO - 8, c), F32)],
        compiler_params=_cparams("arbitrary", "arbitrary"),
        name="conv_module",
    )(u, u, u, wp, cb.reshape(1, c), ln_g.reshape(1, c), ln_b.reshape(1, c))


def _split3(x):
    hi = x.astype(BF16)
    r1 = x - hi.astype(F32)
    mid = r1.astype(BF16)
    lo = (r1 - mid.astype(F32)).astype(BF16)
    return hi, mid, lo


def _hgrn_scores(q, b_ref, kk_ref, d, base, lanes, rev, wsel, sub_iota, lane_iota):
    c = HG_CHUNK
    sb = HG_SUB
    nb = c // sb
    b = b_ref[d, base:base + c, lanes]
    kk = kk_ref[d, base:base + c, lanes]
    rows = []
    for i in range(nb):
        r0 = i * sb
        lo, hi = (r0 + sb, c) if rev else (0, r0)
        if hi <= lo:
            rows.append(jnp.zeros((sb, c), F32))
            continue
        edge = base + (r0 + sb if rev else r0 - 1)
        ref = b_ref[d, edge:edge + 1, lanes]
        qt = q[r0:r0 + sb, :] * jnp.exp2(b[r0:r0 + sb, :] - ref)
        kt = kk[lo:hi, :] * jnp.exp2(ref - b[lo:hi, :])
        pieces = [jnp.zeros((lo, HG_KDIM), F32), kt] if rev else [kt, jnp.zeros((c - hi, HG_KDIM), F32)]
        rows.append(_dot_nt(qt.astype(BF16), jnp.concatenate(pieces, axis=0).astype(BF16)))
    xs = []
    for i in range(nb):
        r0 = i * sb
        bq = b[r0:r0 + sb, :]
        qq = q[r0:r0 + sb, :]
        for p in range(sb // 2):
            pair = []
            for s in (2 * p, 2 * p + 1):
                keep = (sub_iota <= s) if rev else (sub_iota >= s)
                row = base + r0 + s
                e = jnp.exp2(bq - b_ref[d, row:row + 1, lanes])
                pair.append(jnp.where(keep, qq * kk_ref[d, row:row + 1, lanes] * e, 0.0))
            xs.append(jnp.concatenate(pair, axis=1))
    y = _dot(jnp.concatenate(xs, axis=0).astype(BF16), wsel)
    blocks = []
    for i in range(nb):
        r0 = i * sb
        acc = rows[i]
        for p in range(sb // 2):
            g0 = (i * (sb // 2) + p) * sb
            acc = jnp.where((lane_iota >> 1) == (r0 // 2 + p), y[g0:g0 + sb, :c], acc)
        blocks.append(acc)
    return jnp.concatenate(blocks, axis=0)


def _hgrn_kernel(*refs, layer, tc, has_s0, emit_state):
    qf_ref, zf_ref, vf_ref, qb_ref, zb_ref, vb_ref, lbraw_ref = refs[:7]
    pos = 7
    s0_ref = None
    if has_s0:
        s0_ref = refs[pos]
        pos += 1
    of_ref, ob_ref = refs[pos], refs[pos + 1]
    pos += 2
    sf_ref = None
    if emit_state:
        sf_ref = refs[pos]
        pos += 1
    st_ref, b_ref, kk_ref, qs_ref, qe_ref, kd_ref, v16_ref, dec_ref, sc_ref = refs[pos:pos + 9]

    s = pl.program_id(1)
    ns = pl.num_programs(1)
    c = HG_CHUNK
    nch = tc // c
    w = HG_WIDTH

    @pl.when(s == 0)
    def _init():
        for d in range(2):
            for h in range(HG_HEADS):
                if has_s0:
                    st_ref[d, h] = s0_ref[0, d, h].T
                else:
                    st_ref[d, h] = jnp.zeros((HG_VDIM, HG_KDIM), F32)

    def lower_bound(d):
        rows = [lbraw_ref[d, j:j + 1, :] for j in range(DEPTH + 1)]
        m = functools.reduce(jnp.maximum, rows)
        ex = [jnp.exp(r - m) for r in rows]
        tot = functools.reduce(lambda a, b_: a + b_, ex)
        return functools.reduce(lambda a, b_: a + b_, [e / tot for e in ex[:layer + 1]])

    r64 = lax.broadcasted_iota(jnp.int32, (c, c), 0)
    c64 = lax.broadcasted_iota(jnp.int32, (c, c), 1)
    tri = ((r64 >= c64).astype(BF16), (r64 <= c64).astype(BF16))
    rsel = lax.broadcasted_iota(jnp.int32, (2 * LANES, LANES), 0)
    csel = lax.broadcasted_iota(jnp.int32, (2 * LANES, LANES), 1)
    wsel = ((rsel >> 7) == (csel & 1)).astype(BF16)
    sub_iota = lax.broadcasted_iota(jnp.int32, (HG_SUB, LANES), 0)
    lane_iota = lax.broadcasted_iota(jnp.int32, (HG_SUB, c), 1)

    dirs = ((qf_ref, zf_ref, vf_ref, of_ref), (qb_ref, zb_ref, vb_ref, ob_ref))

    sums = {}
    for d, (q_ref, z_ref, v_ref, _) in enumerate(dirs):
        lb = lower_bound(d)
        for ci in range(nch):
            base = ci * c
            f = lb + (1.0 - lb) * jax.nn.sigmoid(z_ref[0, base:base + c, :])
            kk_ref[d, base:base + c, :] = 1.0 - f
            sums[d, ci] = _dot(tri[d], jnp.concatenate(_split3(jnp.log(f) * LOG2_E), axis=1))
            qs_ref[d, base:base + c, :] = _silu(q_ref[0, base:base + c, :])
            v16_ref[d, base:base + c, :] = v_ref[0, base:base + c, :].astype(BF16)
    b_low = None
    for d in range(2):
        for ci in range(nch):
            base = ci * c
            cs = sums[d, ci]
            b = cs[:, :w] + cs[:, w:2 * w] + cs[:, 2 * w:]
            b_ref[d, base:base + c, :] = b
            b_end = b[0:1, :] if d == 1 else b[c - 1:c, :]
            b_low = b_end if b_low is None else jnp.minimum(b_low, b_end)
            qe_ref[d, base:base + c, :] = (qs_ref[d, base:base + c, :] * jnp.exp2(b)).astype(BF16)
            kd_ref[d, base:base + c, :] = (kk_ref[d, base:base + c, :] * jnp.exp2(b_end - b)).astype(BF16)
            dec_ref[d, ci * HG_SUB:ci * HG_SUB + 1, :] = jnp.exp2(b_end)

    chains = [(d, h) for d in range(2) for h in range(HG_HEADS)]

    def recur(score_of):
        states = [st_ref[d, h] for d, h in chains]
        for step in range(nch):
            for n, (d, h) in enumerate(chains):
                ci = nch - 1 - step if d == 1 else step
                sl = slice(h * HG_KDIM, (h + 1) * HG_KDIM)
                rows = slice(ci * c, (ci + 1) * c)
                v16 = v16_ref[d, rows, sl]
                st = states[n]
                dirs[d][3][0, rows, sl] = (_dot_nt(qe_ref[d, rows, sl], st.astype(BF16))
                                           + _dot(score_of(d, ci, h), v16))
                states[n] = st * dec_ref[d, ci * HG_SUB:ci * HG_SUB + 1, sl] + _dot_tn(v16, kd_ref[d, rows, sl])
        return states

    safe = jnp.min(b_low) >= -HG_SAFE_LOG2
    split = {}
    for d in range(2):
        keep = (r64 <= c64) if d == 1 else (r64 >= c64)
        for ci in range(nch):
            rows = slice(ci * c, (ci + 1) * c)
            kinv = (kk_ref[d, rows, :] * jnp.exp2(-b_ref[d, rows, :])).astype(BF16)
            for h in range(HG_HEADS):
                sl = slice(h * HG_KDIM, (h + 1) * HG_KDIM)
                sc = _dot_nt(qe_ref[d, rows, sl], kinv[:, sl])
                split[d, ci, h] = jnp.where(keep, sc, 0.0).astype(BF16)
    split_states = recur(lambda d, ci, h: split[d, ci, h])

    @pl.when(safe)
    def _keep():
        for n, (d, h) in enumerate(chains):
            st_ref[d, h] = split_states[n]

    @pl.when(jnp.logical_not(safe))
    def _exact():
        for d in range(2):
            for ci in range(nch):
                base = ci * c
                for h in range(HG_HEADS):
                    sl = slice(h * HG_KDIM, (h + 1) * HG_KDIM)
                    sc = _hgrn_scores(qs_ref[d, base:base + c, sl], b_ref, kk_ref, d, base, sl, d == 1,
                                      wsel, sub_iota, lane_iota)
                    sc_ref[d, ci, h] = sc.astype(BF16)
        exact_states = recur(lambda d, ci, h: sc_ref[d, ci, h])
        for n, (d, h) in enumerate(chains):
            st_ref[d, h] = exact_states[n]

    if emit_state:
        @pl.when(s == ns - 1)
        def _fin():
            for d in range(2):
                for h in range(HG_HEADS):
                    sf_ref[0, d, h] = st_ref[d, h].T


def _hgrn_scan(q, zf, zb, iv, lb_raw, s0, layer, emit_state):
    b, t, w = q.shape
    tc = min(HG_TILE, t)
    ns = t // tc
    nch = tc // HG_CHUNK
    fwd = pl.BlockSpec((1, tc, w), lambda i, j: (i, j, 0))
    bwd = pl.BlockSpec((1, tc, w), lambda i, j: (i, ns - 1 - j, 0))
    in_specs = [fwd, fwd, fwd, bwd, bwd, bwd, pl.BlockSpec(lb_raw.shape, lambda i, j: (0, 0, 0))]
    args = [q, zf, iv, q, zb, iv, lb_raw]
    state_spec = pl.BlockSpec((1, 2, HG_HEADS, HG_KDIM, HG_VDIM), lambda i, j: (i, 0, 0, 0, 0))
    if s0 is not None:
        in_specs.append(state_spec)
        args.append(s0)
    out_specs = [fwd, bwd]
    out_shape = [jax.ShapeDtypeStruct((b, t, w), F32), jax.ShapeDtypeStruct((b, t, w), F32)]
    if emit_state:
        out_specs.append(state_spec)
        out_shape.append(jax.ShapeDtypeStruct((b, 2, HG_HEADS, HG_KDIM, HG_VDIM), F32))
    return pl.pallas_call(
        functools.partial(_hgrn_kernel, layer=layer, tc=tc, has_s0=s0 is not None, emit_state=emit_state),
        grid=(b, ns),
        in_specs=in_specs,
        out_specs=out_specs,
        out_shape=out_shape,
        scratch_shapes=[pltpu.VMEM((2, HG_HEADS, HG_VDIM, HG_KDIM), F32),
                        pltpu.VMEM((2, tc, w), F32),
                        pltpu.VMEM((2, tc, w), F32),
                        pltpu.VMEM((2, tc, w), F32),
                        pltpu.VMEM((2, tc, w), BF16),
                        pltpu.VMEM((2, tc, w), BF16),
                        pltpu.VMEM((2, tc, w), BF16),
                        pltpu.VMEM((2, nch * HG_SUB, w), F32),
                        pltpu.VMEM((2, nch, HG_HEADS, HG_CHUNK, HG_CHUNK), BF16)],
        compiler_params=_cparams("arbitrary", "arbitrary"),
        name="hgrn_scan",
    )(*args)


def _head_norm(x, gain, seg_ones):
    ss = _dot((x * x).astype(BF16), seg_ones)
    return x * lax.rsqrt(ss * (1.0 / HEAD_DIM) + EPS) * gain


def _rope(x, cos, sin):
    lane = lax.broadcasted_iota(jnp.int32, (1, LANES), 1)
    first = (lane & 31) < 16
    parts = []
    for j in range(x.shape[1] // LANES):
        xs = x[:, LANES * j:LANES * (j + 1)]
        partner = jnp.where(first, pltpu.roll(xs, LANES - 16, 1), pltpu.roll(xs, 16, 1))
        parts.append(xs * cos + partner * sin)
    return jnp.concatenate(parts, axis=1)


def _qkv_kernel(*refs, rope):
    x_ref, mod_ref, g_ref, w_ref, qn_ref, kn_ref = refs[:6]
    pos = 6
    if rope:
        cos_ref, sin_ref = refs[6], refs[7]
        pos = 8
    q_ref, k_ref, v_ref = refs[pos:pos + 3]
    h = _prenorm(x_ref[...], g_ref[...], mod_ref, 1).astype(BF16)
    nq = N_HEADS * HEAD_DIM
    nk = N_KV_HEADS * HEAD_DIM
    sw = 256
    cw = 2 * sw
    r = lax.broadcasted_iota(jnp.int32, (sw, sw), 0)
    c = lax.broadcasted_iota(jnp.int32, (sw, sw), 1)
    seg_ones = ((r >> 6) == (c >> 6)).astype(BF16)
    for c0 in range(0, nq + 2 * nk, cw):
        y2 = _dot(h, w_ref[:, c0:c0 + cw])
        for c1 in range(c0, c0 + cw, sw):
            y = y2[:, c1 - c0:c1 - c0 + sw]
            if c1 >= nq + nk:
                v_ref[:, c1 - nq - nk:c1 - nq - nk + sw] = y
                continue
            is_q = c1 < nq
            gain = qn_ref[:, c1:c1 + sw] if is_q else kn_ref[:, c1 - nq:c1 - nq + sw]
            y = _head_norm(y, gain, seg_ones)
            if rope:
                y = _rope(y, cos_ref[...], sin_ref[...])
            if is_q:
                q_ref[:, c1:c1 + sw] = y * (HEAD_DIM ** -0.5 * LOG2_E)
            else:
                k_ref[:, c1 - nq:c1 - nq + sw] = y


def _qkv(x, mod, g, w, qn, kn, rope_tabs):
    n, d = x.shape
    tm = FFN_ROW_TILE
    per_group = (n // mod.shape[0]) // tm
    nq = N_HEADS * HEAD_DIM
    nk = N_KV_HEADS * HEAD_DIM
    in_specs = [pl.BlockSpec((tm, d), lambda i: (i, 0)),
                pl.BlockSpec((1, N_MOD, d), lambda i: (i // per_group, 0, 0)),
                pl.BlockSpec((1, d), lambda i: (0, 0)),
                _resident(w.shape),
                pl.BlockSpec((1, nq), lambda i: (0, 0)),
                pl.BlockSpec((1, nk), lambda i: (0, 0))]
    args = [x, mod, g.reshape(1, d), w, jnp.tile(qn, N_HEADS).reshape(1, nq), jnp.tile(kn, N_KV_HEADS).reshape(1, nk)]
    if rope_tabs is not None:
        cos, sin = rope_tabs
        per_seq = cos.shape[0] // tm
        in_specs += [pl.BlockSpec((tm, LANES), lambda i: (i % per_seq, 0))] * 2
        args += [cos, sin]
    return pl.pallas_call(
        functools.partial(_qkv_kernel, rope=rope_tabs is not None),
        grid=(n // tm,),
        in_specs=in_specs,
        out_specs=[pl.BlockSpec((tm, nq), lambda i: (i, 0)),
                   pl.BlockSpec((tm, nk), lambda i: (i, 0)),
                   pl.BlockSpec((tm, nk), lambda i: (i, 0))],
        out_shape=[jax.ShapeDtypeStruct((n, nq), F32),
                   jax.ShapeDtypeStruct((n, nk), F32),
                   jax.ShapeDtypeStruct((n, nk), F32)],
        compiler_params=_cparams("arbitrary"),
        name="qkv_proj",
    )(*args)


def _rope_tables(t):
    lane = jnp.arange(LANES)
    inv = ROPE_BASE ** (-jnp.arange(0, ROPE_AX, 2, dtype=F32) / ROPE_AX)
    freq = inv[lane % (ROPE_AX // 2)][None, :]
    rows = t // GRID_W
    ang_r = jnp.arange(rows, dtype=F32)[:, None] * freq
    ang_c = jnp.arange(GRID_W, dtype=F32)[:, None] * freq
    by_row = lambda a: jnp.repeat(a, GRID_W, axis=0)
    by_col = lambda a: jnp.tile(a, (rows, 1))
    row_lane = (lane % HEAD_DIM < ROPE_AX)[None, :]
    cos = jnp.where(row_lane, by_row(jnp.cos(ang_r)), by_col(jnp.cos(ang_c)))
    sin = jnp.where(row_lane, by_row(jnp.sin(ang_r)), by_col(jnp.sin(ang_c)))
    return cos, jnp.where((lane % ROPE_AX < ROPE_AX // 2)[None, :], -sin, sin)


def _attend(q, sections, sink_ref, tq):
    lane = lax.broadcasted_iota(jnp.int32, (1, LANES), 1)
    low = lane < HEAD_DIM
    kall = jnp.concatenate([sec[0] for sec in sections], axis=0)
    vall = jnp.concatenate([sec[1] for sec in sections], axis=0)
    ones = jnp.ones((kall.shape[0], LANES), BF16)
    nslab = N_KV_HEADS // 2
    k16 = [kall[:, LANES * j:LANES * (j + 1)].astype(BF16) for j in range(nslab)]
    vext = [jnp.concatenate([vall[:, LANES * j:LANES * (j + 1)].astype(BF16), ones], axis=1) for j in range(nslab)]
    halves = (low, jnp.logical_not(low))

    scores, sinks, probs, tops, outs = {}, {}, {}, {}, {}

    def score(kv):
        qs, sk = [], []
        for g in range(GROUP):
            h = GROUP * kv + g
            slab = q[:, LANES * (h // 2):LANES * (h // 2 + 1)]
            if h % 2 != kv % 2:
                slab = pltpu.roll(slab, HEAD_DIM, 1)
            qs.append(jnp.where(halves[kv % 2], slab, 0.0).astype(BF16))
            sk.append(jnp.full((tq, LANES), sink_ref[h] * LOG2_E, F32))
        scores[kv] = _dot_nt(jnp.concatenate(qs, axis=0), k16[kv // 2])
        sinks[kv] = jnp.concatenate(sk, axis=0)

    def soften(kv):
        pieces, c0 = [], 0
        for k_sec, _, bias in sections:
            for c1 in range(c0, c0 + k_sec.shape[0], LANES):
                piece = scores[kv][:, c1:c1 + LANES]
                if bias is not None:
                    piece = piece + jnp.concatenate([bias[:, c1 - c0:c1 - c0 + LANES]] * GROUP, axis=0)
                pieces.append(piece)
            c0 += k_sec.shape[0]
        top = functools.reduce(jnp.maximum, pieces)
        m = jnp.maximum(jnp.broadcast_to(jnp.max(top, axis=-1, keepdims=True), top.shape), sinks[kv])
        probs[kv] = jnp.concatenate([jnp.exp2(piece - m).astype(BF16) for piece in pieces], axis=1)
        tops[kv] = m

    def weigh(kv):
        outs[kv] = _dot(probs[kv], vext[kv // 2])

    for stage in (score, soften, weigh):
        for kv in range(N_KV_HEADS):
            stage(kv)

    slabs = []
    for kv in range(N_KV_HEADS):
        oe = outs[kv]
        o = oe[:, :LANES] * (1.0 / (oe[:, LANES:] + jnp.exp2(sinks[kv] - tops[kv])))
        heads = []
        for g in range(GROUP):
            og = o[g * tq:(g + 1) * tq]
            heads.append(og if g % 2 == kv % 2 else pltpu.roll(og, HEAD_DIM, 1))
        slabs.append(jnp.where(low, heads[0], heads[1]))
        slabs.append(jnp.where(low, heads[2], heads[3]))
    return jnp.concatenate(slabs, axis=1)


def _ctx_attn_kernel(sink_ref, q_ref, k_ref, v_ref, o_ref, *, tq):
    o_ref[...] = _attend(q_ref[...], [(k_ref[...], v_ref[...], None)], sink_ref, tq)


def _ctx_attention(q, k, v, sinks, seq):
    n = q.shape[0]
    nq = q.shape[1]
    nk = k.shape[1]
    return pl.pallas_call(
        functools.partial(_ctx_attn_kernel, tq=seq),
        grid=(n // seq,),
        in_specs=[pl.BlockSpec(memory_space=pltpu.SMEM),
                  pl.BlockSpec((seq, nq), lambda i: (i, 0)),
                  pl.BlockSpec((seq, nk), lambda i: (i, 0)),
                  pl.BlockSpec((seq, nk), lambda i: (i, 0))],
        out_specs=pl.BlockSpec((seq, nq), lambda i: (i, 0)),
        out_shape=jax.ShapeDtypeStruct((n, nq), F32),
        compiler_params=_cparams("arbitrary"),
        name="ctx_attention",
    )(sinks, q, k, v)


def _lat_attn_kernel(sink_ref, q_ref, ck_ref, cv_ref, kp_ref, kc_ref, kn_ref, vp_ref, vc_ref, vn_ref, o_ref, *, tq):
    n = pl.program_id(1)
    nb = pl.num_programs(1)
    qi = lax.broadcasted_iota(jnp.int32, (tq, tq), 0)
    kj = lax.broadcasted_iota(jnp.int32, (tq, tq), 1)
    bias_prev = jnp.where((kj >= qi) & (n > 0), 0.0, -jnp.inf)
    bias_next = jnp.where((kj <= qi) & (n < nb - 1), 0.0, -jnp.inf)
    sections = [(ck_ref[0], cv_ref[0], None), (kp_ref[...], vp_ref[...], bias_prev),
                (kc_ref[...], vc_ref[...], None), (kn_ref[...], vn_ref[...], bias_next)]
    o_ref[...] = _attend(q_ref[...], sections, sink_ref, tq)


def _lat_attention(q, k, v, ck, cv, sinks, seq):
    n = q.shape[0]
    nq = q.shape[1]
    nk = k.shape[1]
    tq = ATT_BLOCK
    nb = seq // tq
    nctx = ck.shape[1]
    blk = lambda shift: pl.BlockSpec((tq, nk), lambda b, j: (b * nb + jnp.clip(j + shift, 0, nb - 1), 0))
    ctx = pl.BlockSpec((1, nctx, nk), lambda b, j: (b, 0, 0))
    return pl.pallas_call(
        functools.partial(_lat_attn_kernel, tq=tq),
        grid=(n // seq, nb),
        in_specs=[pl.BlockSpec(memory_space=pltpu.SMEM),
                  pl.BlockSpec((tq, nq), lambda b, j: (b * nb + j, 0)),
                  ctx, ctx, blk(-1), blk(0), blk(1), blk(-1), blk(0), blk(1)],
        out_specs=pl.BlockSpec((tq, nq), lambda b, j: (b * nb + j, 0)),
        out_shape=jax.ShapeDtypeStruct((n, nq), F32),
        compiler_params=_cparams("arbitrary", "arbitrary"),
        name="lat_attention",
    )(sinks, q, ck, cv, k, k, k, v, v, v)


def kernel(x_prompt, x_sample, c, state_hgrn, cache_k, cache_v, c_ctx, w_mod, b_mod, norm_g, ffn_w_in, ffn_w_out, ev_w_in, ev_w_out, conv_w, conv_b, conv_ln_g, conv_ln_b, hg_lb_raw, hg_norm_g, od_w_in, od_w_out, q_norm_g, k_norm_g, sinks):
    batch, seq, d = x_prompt.shape
    dec_batch, dec_seq, _ = x_sample.shape
    depth = w_mod.shape[0]
    ffn_in = ffn_w_in.astype(BF16)
    ffn_out = ffn_w_out.astype(BF16)
    ev_in = ev_w_in.astype(BF16)
    ev_out = ev_w_out.astype(BF16)
    od_in = od_w_in.astype(BF16)
    od_out = od_w_out.astype(BF16)

    cond = jnp.concatenate([c_ctx[None, :], c, jnp.zeros((8 - 1 - dec_batch, d), F32)], axis=0)
    mod = _adaln(cond, w_mod, b_mod).reshape(depth, 8, N_MOD, d)
    rope_tabs = _rope_tables(dec_seq)

    streams = [
        dict(x=x_prompt.reshape(batch * seq, d), nb=batch, t=seq, lo=0, hi=1, latent=False),
        dict(x=x_sample.reshape(dec_batch * dec_seq, d), nb=dec_batch, t=dec_seq, lo=1, hi=1 + dec_batch, latent=True),
    ]
    hg_states, ks_new, vs_new = [], [], []
    for st in streams:
        x = st["x"]
        nb, t, latent = st["nb"], st["t"], st["latent"]
        for l in range(depth):
            m = mod[l, st["lo"]:st["hi"]]
            x = _ffn(x, m, norm_g[l, 0], ffn_in, ffn_out, (l, 0), 0)
            if l % 2 == 0:
                e = l // 2
                glu, hq, hzf, hzb, hi, hgz = _proj(x, m, norm_g[l, 1], ev_in[e])
                a = _conv_module(glu.reshape(nb, t, 2 * CONV_CH), conv_w[e], conv_b[e], conv_ln_g[e], conv_ln_b[e])
                s0 = state_hgrn[:, e] if latent else None
                seqs = [z.reshape(nb, t, HG_WIDTH) for z in (hq, hzf, hzb, hi)]
                res = _hgrn_scan(*seqs, hg_lb_raw, s0, l, emit_state=not latent)
                if not latent:
                    hg_states.append(res[2])
                n = nb * t
                tail = ("even", (a.reshape(n, CONV_CH), res[0].reshape(n, HG_WIDTH), res[1].reshape(n, HG_WIDTH),
                                 hgz, hg_norm_g[e], ev_out[e]))
            else:
                o = l // 2
                q, k, v = _qkv(x, m, norm_g[l, 1], od_in[o], q_norm_g[o], k_norm_g[o], rope_tabs if latent else None)
                if latent:
                    nk = N_KV_HEADS * HEAD_DIM
                    ck = cache_k[:, o].reshape(nb, -1, nk)
                    cv = cache_v[:, o].reshape(nb, -1, nk)
                    att = _lat_attention(q, k, v, ck, cv, sinks[o], t)
                else:
                    att = _ctx_attention(q, k, v, sinks[o], t)
                    ks_new.append(k.reshape(nb, t, N_KV_HEADS, HEAD_DIM))
                    vs_new.append(v.reshape(nb, t, N_KV_HEADS, HEAD_DIM))
                tail = ("attn", (att, od_out[o]))
            x = _ffn(x, m, norm_g[l, 2], ffn_in, ffn_out, (l, 1), 2, *tail)
        st["y"] = x

    y_prompt = streams[0]["y"].reshape(batch, seq, d)
    y_sample = streams[1]["y"].reshape(dec_batch, dec_seq, d)
    new_state_hgrn = jnp.stack(hg_states, axis=1)
    new_cache_k = jnp.stack(ks_new, axis=1)
    new_cache_v = jnp.stack(vs_new, axis=1)
    return (y_prompt, y_sample, new_state_hgrn, new_cache_k, new_cache_v)
```
